```python
import math
import jax, jax.numpy as jnp
from jax import lax
import numpy as np

D_MODEL = 1024
BATCH = 4
SEQ = 4096
DEPTH = 4
DEC_BATCH = 128
DEC_SEQ = 1
PAST_LEN = 2048
PAGE_SIZE = 128

N_A = DEPTH // 2
N_B = DEPTH - N_A
HEAD_DIM = 64
N_HEADS = D_MODEL // (2 * HEAD_DIM)
QK_DIM = N_HEADS * 2 * HEAD_DIM
V_DIM = N_HEADS * 2 * HEAD_DIM
D_FF = ((8 * D_MODEL // 3 + 255) // 256) * 256
CONV_WIDTH = 31
ROPE_THETA = 10000.0
Q_BLOCK = 128
EPS = 1e-6
LAMBDA_STD = 0.1

kernel_name = "yoco_conformer_diffattn_step"


def rms_norm(x, g):
    xf = x.astype(jnp.float32)
    y = xf * lax.rsqrt(jnp.mean(xf * xf, axis=-1, keepdims=True) + EPS)
    return (y * g.astype(jnp.float32)).astype(x.dtype)


def layer_norm(x, g, b):
    xf = x.astype(jnp.float32)
    mu = jnp.mean(xf, axis=-1, keepdims=True)
    xc = xf - mu
    y = xc * lax.rsqrt(jnp.mean(xc * xc, axis=-1, keepdims=True) + EPS)
    return (y * g.astype(jnp.float32) + b.astype(jnp.float32)).astype(x.dtype)


def swiglu_ffn(x, w_in, w_out):
    gate, up = jnp.split(x @ w_in, 2, axis=-1)
    return (jax.nn.silu(gate) * up) @ w_out


def rope(x, pos):
    half = HEAD_DIM // 2
    freqs = ROPE_THETA ** (-jnp.arange(half, dtype=jnp.float32) * 2.0 / HEAD_DIM)
    ang = pos.astype(jnp.float32)[:, None] * freqs[None, :]
    cos = jnp.cos(ang)[None, :, None, None, :]
    sin = jnp.sin(ang)[None, :, None, None, :]
    xf = x.astype(jnp.float32)
    x1, x2 = xf[..., :half], xf[..., half:]
    out = jnp.concatenate([x1 * cos - x2 * sin, x2 * cos + x1 * sin], axis=-1)
    return out.astype(x.dtype)


def conv_module(x_n, prev, w_in, b_in, dw, dw_b, ln_g, ln_b, w_out, b_out):
    a, g = jnp.split(x_n @ w_in + b_in, 2, axis=-1)
    u = a * jax.nn.sigmoid(g)
    full = jnp.concatenate([prev.astype(u.dtype), u], axis=1)
    c = lax.conv_general_dilated(
        full, dw[:, None, :].astype(u.dtype), window_strides=(1,), padding='VALID',
        dimension_numbers=('NWC', 'WIO', 'NWC'), feature_group_count=D_MODEL)
    c = layer_norm(c + dw_b, ln_g, ln_b)
    out = jax.nn.silu(c) @ w_out + b_out
    return out, full[:, -(CONV_WIDTH - 1):]


def diff_attention(q, q_pos, k, v, k_pos, lam):
    b, nq = q.shape[0], q.shape[1]
    blk = Q_BLOCK if nq % Q_BLOCK == 0 else nq
    nb = nq // blk
    qb = q.reshape(b, nb, blk, N_HEADS, 2, HEAD_DIM).transpose(1, 0, 2, 3, 4, 5)
    pb = q_pos.reshape(nb, blk)
    scale = HEAD_DIM ** -0.5

    def one_block(args):
        qi, pi = args
        s = jnp.einsum('bqhcd,bkhcd->bhcqk', qi, k).astype(jnp.float32) * scale
        mask = k_pos[None, :] <= pi[:, None]
        s = jnp.where(mask, s, -jnp.inf)
        p = jax.nn.softmax(s, axis=-1)
        a = p[:, :, 0] - lam * p[:, :, 1]
        return jnp.einsum('bhqk,bkhe->bqhe', a.astype(v.dtype), v)

    o = lax.map(one_block, (qb, pb))
    return o.transpose(1, 0, 2, 3, 4).reshape(b, nq, N_HEADS, 2 * HEAD_DIM)


def setup_inputs(seed: int = 0) -> dict:
    key = jax.random.key(seed)
    ks = jax.random.split(key, 32)
    n_pages = PAST_LEN // PAGE_SIZE
    n_pool = (DEC_BATCH * n_pages * 5) // 4
    f32 = jnp.float32
    nrm = lambda k, shape, s: jax.random.normal(k, shape, f32) * s
    gain = lambda k, shape: 1.0 + 0.02 * jax.random.normal(k, shape, f32)
    page_table = jax.random.permutation(ks[5], n_pool)[:DEC_BATCH * n_pages]
    page_table = page_table.reshape(DEC_BATCH, n_pages).astype(jnp.int32)
    return {
        "x_prompt": nrm(ks[0], (BATCH, SEQ, D_MODEL), 1.0),
        "x_sample": nrm(ks[1], (DEC_BATCH, DEC_SEQ, D_MODEL), 1.0),
        "state_conv": nrm(ks[2], (N_A, DEC_BATCH, CONV_WIDTH - 1, D_MODEL), 0.5),
        "cache_k": nrm(ks[3], (n_pool, PAGE_SIZE, N_HEADS, 2 * HEAD_DIM), 1.0),
        "cache_v": nrm(ks[4], (n_pool, PAGE_SIZE, N_HEADS, 2 * HEAD_DIM), 1.0),
        "page_table": page_table,
        "norm_ffn": gain(ks[6], (DEPTH, 2, D_MODEL)),
        "ffn_w_in": nrm(ks[7], (DEPTH, 2, D_MODEL, 2 * D_FF), D_MODEL ** -0.5),
        "ffn_w_out": nrm(ks[8], (DEPTH, 2, D_FF, D_MODEL), D_FF ** -0.5),
        "norm_mix": gain(ks[9], (DEPTH, D_MODEL)),
        "conv_w_in": nrm(ks[10], (N_A, D_MODEL, 2 * D_MODEL), D_MODEL ** -0.5),
        "conv_b_in": nrm(ks[11], (N_A, 2 * D_MODEL), 0.02),
        "conv_dw": nrm(ks[12], (N_A, CONV_WIDTH, D_MODEL), CONV_WIDTH ** -0.5),
        "conv_dw_b": nrm(ks[13], (N_A, D_MODEL), 0.02),
        "conv_ln_g": gain(ks[14], (N_A, D_MODEL)),
        "conv_ln_b": nrm(ks[15], (N_A, D_MODEL), 0.02),
        "conv_w_out": nrm(ks[16], (N_A, D_MODEL, D_MODEL), D_MODEL ** -0.5),
        "conv_b_out": nrm(ks[17], (N_A, D_MODEL), 0.02),
        "kv_norm": gain(ks[18], (D_MODEL,)),
        "w_kv": nrm(ks[19], (D_MODEL, QK_DIM + V_DIM), D_MODEL ** -0.5),
        "w_q": nrm(ks[20], (N_B, D_MODEL, QK_DIM), D_MODEL ** -0.5),
        "lambda_q1": nrm(ks[21], (N_B, HEAD_DIM), LAMBDA_STD),
        "lambda_k1": nrm(ks[22], (N_B, HEAD_DIM), LAMBDA_STD),
        "lambda_q2": nrm(ks[23], (N_B, HEAD_DIM), LAMBDA_STD),
        "lambda_k2": nrm(ks[24], (N_B, HEAD_DIM), LAMBDA_STD),
        "subln_g": gain(ks[25], (N_B, 2 * HEAD_DIM)),
        "w_o": nrm(ks[26], (N_B, V_DIM, D_MODEL), V_DIM ** -0.5),
        "norm_final": gain(ks[27], (D_MODEL,)),
    }


def reference(x_prompt, x_sample, state_conv, cache_k, cache_v, page_table,
              norm_ffn, ffn_w_in, ffn_w_out, norm_mix,
              conv_w_in, conv_b_in, conv_dw, conv_dw_b, conv_ln_g, conv_ln_b,
              conv_w_out, conv_b_out, kv_norm, w_kv, w_q,
              lambda_q1, lambda_k1, lambda_q2, lambda_k2, subln_g, w_o, norm_final):

    def run(x, pos, conv_prev, past_k, past_v):
        b, s = x.shape[0], x.shape[1]
        conv_states = []
        k_new = v_new = k_all = v_all = k_pos = None
        for l in range(DEPTH):
            if l == N_A:
                kv = rms_norm(x, kv_norm) @ w_kv
                k_new = rope(kv[..., :QK_DIM].reshape(b, s, N_HEADS, 2, HEAD_DIM), pos)
                k_new = k_new.reshape(b, s, N_HEADS, 2 * HEAD_DIM)
                v_new = kv[..., QK_DIM:].reshape(b, s, N_HEADS, 2 * HEAD_DIM)
                if past_k is None:
                    k_all, v_all, k_pos = k_new, v_new, pos
                else:
                    k_all = jnp.concatenate([past_k.astype(k_new.dtype), k_new], axis=1)
                    v_all = jnp.concatenate([past_v.astype(v_new.dtype), v_new], axis=1)
                    k_pos = jnp.arange(k_all.shape[1], dtype=jnp.int32)
                k_all = k_all.reshape(b, -1, N_HEADS, 2, HEAD_DIM)
            x = x + 0.5 * swiglu_ffn(rms_norm(x, norm_ffn[l, 0]), ffn_w_in[l, 0], ffn_w_out[l, 0])
            h = rms_norm(x, norm_mix[l])
            if l < N_A:
                mix, st = conv_module(h, conv_prev[l], conv_w_in[l], conv_b_in[l],
                                      conv_dw[l], conv_dw_b[l], conv_ln_g[l], conv_ln_b[l],
                                      conv_w_out[l], conv_b_out[l])
                conv_states.append(st)
            else:
                j = l - N_A
                lam_init = 0.8 - 0.6 * math.exp(-0.3 * l)
                lam = (jnp.exp(jnp.sum(lambda_q1[j].astype(jnp.float32) * lambda_k1[j].astype(jnp.float32)))
                       - jnp.exp(jnp.sum(lambda_q2[j].astype(jnp.float32) * lambda_k2[j].astype(jnp.float32)))
                       + lam_init)
                q = rope((h @ w_q[j]).reshape(b, s, N_HEADS, 2, HEAD_DIM), pos)
                o = diff_attention(q, pos, k_all, v_all, k_pos, lam)
                o = rms_norm(o, subln_g[j]) * (1.0 - lam_init)
                mix = o.reshape(b, s, V_DIM) @ w_o[j]
            x = x + mix
            x = x + 0.5 * swiglu_ffn(rms_norm(x, norm_ffn[l, 1]), ffn_w_in[l, 1], ffn_w_out[l, 1])
        return rms_norm(x, norm_final), jnp.stack(conv_states, axis=0), k_new, v_new

    pos_p = jnp.arange(x_prompt.shape[1], dtype=jnp.int32)
    prev_p = jnp.zeros((N_A, x_prompt.shape[0], CONV_WIDTH - 1, D_MODEL), x_prompt.dtype)
    y_prompt, conv_p, k_p, v_p = run(x_prompt, pos_p, prev_p, None, None)

    n_dec, past_len = page_table.shape[0], page_table.shape[1] * PAGE_SIZE
    past_k = cache_k[page_table].reshape(n_dec, past_len, N_HEADS, 2 * HEAD_DIM)
    past_v = cache_v[page_table].reshape(n_dec, past_len, N_HEADS, 2 * HEAD_DIM)
    pos_s = past_len + jnp.arange(x_sample.shape[1], dtype=jnp.int32)
    y_sample, conv_s, k_s, v_s = run(x_sample, pos_s, state_conv, past_k, past_v)

    return (y_prompt, y_sample, conv_p, k_p, v_p, conv_s, k_s, v_s)
```

```python
import functools
import math

import jax
import jax.numpy as jnp
from jax import lax
from jax.experimental import pallas as pl
from jax.experimental.pallas import tpu as pltpu

HEAD_DIM = 64
HEAD_W = 2 * HEAD_DIM
CONV_WIDTH = 31
CONV_HALO = 32
ROPE_THETA = 10000.0
EPS = 1e-6
NEG_BIG = -1e30

TOKEN_TILE = 512
FFN_CHUNK = 256
Q_TILE = 256
KV_TILE = 512
CONV_ROWS = 8
CONV_TAIL = CONV_ROWS
VMEM_LIMIT = 56 * 1024 * 1024

F32 = jnp.float32
BF16 = jnp.bfloat16


def _params(n_axes, vmem=VMEM_LIMIT):
    return pltpu.CompilerParams(dimension_semantics=("arbitrary",) * n_axes,
                                vmem_limit_bytes=vmem)


def _rms(x, g):
    return x * lax.rsqrt(jnp.mean(x * x, axis=-1, keepdims=True) + EPS) * g


def _resident(shape, index_map):
    return pl.BlockSpec(shape, index_map, pipeline_mode=pl.Buffered(1))


def _ffn_kernel(x_ref, g_ref, wg_ref, wu_ref, wo_ref, *rest, final_norm):
    if final_norm:
        gf_ref, o_ref, xn_ref, h_ref = rest
    else:
        o_ref, xn_ref, h_ref = rest
    x = x_ref[...]
    xn_ref[...] = _rms(x, g_ref[...]).astype(BF16)
    d_ff = h_ref.shape[1]
    for c in range(0, d_ff, FFN_CHUNK):
        xn = xn_ref[...]
        gate = jnp.dot(xn, wg_ref[:, c:c + FFN_CHUNK], preferred_element_type=F32)
        up = jnp.dot(xn, wu_ref[:, c:c + FFN_CHUNK], preferred_element_type=F32)
        h_ref[:, c:c + FFN_CHUNK] = (gate * jax.nn.sigmoid(gate) * up).astype(BF16)
    y = x + 0.5 * jnp.dot(h_ref[...], wo_ref[...], preferred_element_type=F32)
    if final_norm:
        y = _rms(y, gf_ref[...])
    o_ref[...] = y


def _ffn(x, gains, w_in, w_out, l, i, final_gain=None):
    t, d = x.shape
    f = w_out.shape[2]
    assert f % FFN_CHUNK == 0
    tm = min(TOKEN_TILE, t)
    assert t % tm == 0
    in_specs = [
        pl.BlockSpec((tm, d), lambda r: (r, 0)),
        _resident((None, None, 1, d), lambda r: (l, i, 0, 0)),
        _resident((None, None, d, f), lambda r: (l, i, 0, 0)),
        _resident((None, None, d, f), lambda r: (l, i, 0, 1)),
        _resident((None, None, f, d), lambda r: (l, i, 0, 0)),
    ]
    args = [x, gains, w_in, w_in, w_out]
    if final_gain is not None:
        in_specs.append(_resident((1, d), lambda r: (0, 0)))
        args.append(final_gain)
    return pl.pallas_call(
        functools.partial(_ffn_kernel, final_norm=final_gain is not None),
        grid=(t // tm,),
        in_specs=in_specs,
        out_specs=pl.BlockSpec((tm, d), lambda r: (r, 0)),
        out_shape=jax.ShapeDtypeStruct((t, d), F32),
        scratch_shapes=[pltpu.VMEM((tm, d), BF16), pltpu.VMEM((tm, f), BF16)],
        compiler_params=_params(1),
        name="ffn",
    )(*args)


def _glu_kernel(x_ref, g_ref, w_ref, b_ref, u_ref):
    d = u_ref.shape[1]
    xn = _rms(x_ref[...], g_ref[...]).astype(BF16)
    y = jnp.dot(xn, w_ref[...], preferred_element_type=F32) + b_ref[...]
    u_ref[...] = y[:, :d] * jax.nn.sigmoid(y[:, d:])


def _glu_proj(x, gains, w, b, l):
    t, d = x.shape
    tm = min(TOKEN_TILE, t)
    return pl.pallas_call(
        _glu_kernel,
        grid=(t // tm,),
        in_specs=[
            pl.BlockSpec((tm, d), lambda r: (r, 0)),
            _resident((None, 1, d), lambda r: (l, 0, 0)),
            _resident((None, d, 2 * d), lambda r: (l, 0, 0)),
            _resident((None, 1, 2 * d), lambda r: (l, 0, 0)),
        ],
        out_specs=pl.BlockSpec((tm, d), lambda r: (r, 0)),
        out_shape=jax.ShapeDtypeStruct((t, d), F32),
        compiler_params=_params(1),
        name="conv_glu",
    )(x, gains, w, b)


def _rope_head(xh, cos, sin_signed):
    half = HEAD_DIM // 2
    lane = lax.broadcasted_iota(jnp.int32, xh.shape, 1)
    partner = jnp.where(lane % HEAD_DIM < half,
                        pltpu.roll(xh, HEAD_W - half, 1),
                        pltpu.roll(xh, half, 1))
    return xh * cos + partner * sin_signed


def _kv_kernel(x_ref, g_ref, w_ref, cos_ref, sin_ref, k_ref, v_ref, kb_ref, vb_ref):
    d = k_ref.shape[1]
    xn = _rms(x_ref[...], g_ref[...]).astype(BF16)
    y = jnp.dot(xn, w_ref[...], preferred_element_type=F32)
    v = y[:, d:]
    v_ref[...] = v
    vb_ref[...] = v.astype(BF16)
    cos = cos_ref[...]
    sin = sin_ref[...]
    for h in range(d // HEAD_W):
        sl = slice(h * HEAD_W, (h + 1) * HEAD_W)
        kh = _rope_head(y[:, sl], cos, sin)
        k_ref[:, sl] = kh
        kb_ref[:, sl] = kh.astype(BF16)


def _kv_proj(x, gain, w, cos, sin, pos_tiles):
    t, d = x.shape
    tm = min(TOKEN_TILE, t)
    tok = pl.BlockSpec((tm, d), lambda r: (r, 0))
    pos = pl.BlockSpec((tm, HEAD_W), lambda r: (r % pos_tiles, 0))
    return pl.pallas_call(
        _kv_kernel,
        grid=(t // tm,),
        in_specs=[tok, _resident((1, d), lambda r: (0, 0)),
                  _resident((d, 2 * d), lambda r: (0, 0)), pos, pos],
        out_specs=[tok, tok, tok, tok],
        out_shape=[jax.ShapeDtypeStruct((t, d), F32), jax.ShapeDtypeStruct((t, d), F32),
                   jax.ShapeDtypeStruct((t, d), BF16), jax.ShapeDtypeStruct((t, d), BF16)],
        compiler_params=_params(1),
        name="kv_proj",
    )(x, gain, w, cos, sin)


def _q_kernel(x_ref, g_ref, w_ref, cos_ref, sin_ref, q_ref):
    d = q_ref.shape[1]
    xn = _rms(x_ref[...], g_ref[...]).astype(BF16)
    y = jnp.dot(xn, w_ref[...], preferred_element_type=F32)
    cos = cos_ref[...]
    sin = sin_ref[...]
    scale = HEAD_DIM ** -0.5
    for h in range(d // HEAD_W):
        sl = slice(h * HEAD_W, (h + 1) * HEAD_W)
        q_ref[:, sl] = (_rope_head(y[:, sl], cos, sin) * scale).astype(q_ref.dtype)


def _q_proj(x, gains, w, cos, sin, pos_tiles, l, j, out_dtype):
    t, d = x.shape
    tm = min(TOKEN_TILE, t)
    tok = pl.BlockSpec((tm, d), lambda r: (r, 0))
    pos = pl.BlockSpec((tm, HEAD_W), lambda r: (r % pos_tiles, 0))
    return pl.pallas_call(
        _q_kernel,
        grid=(t // tm,),
        in_specs=[tok, _resident((None, 1, d), lambda r: (l, 0, 0)),
                  _resident((None, d, d), lambda r: (j, 0, 0)), pos, pos],
        out_specs=tok,
        out_shape=jax.ShapeDtypeStruct((t, d), out_dtype),
        compiler_params=_params(1),
        name="q_proj",
    )(x, gains, w, cos, sin)


def _out_kernel(a_ref, w_ref, x_ref, o_ref):
    o_ref[...] = x_ref[...] + jnp.dot(a_ref[...], w_ref[...], preferred_element_type=F32)


def _out_proj(a, w, x, j):
    t, d = x.shape
    tm = min(TOKEN_TILE, t)
    tok = pl.BlockSpec((tm, d), lambda r: (r, 0))
    return pl.pallas_call(
        _out_kernel,
        grid=(t // tm,),
        in_specs=[tok, _resident((None, d, d), lambda r: (j, 0, 0)), tok],
        out_specs=tok,
        out_shape=jax.ShapeDtypeStruct((t, d), F32),
        compiler_params=_params(1),
        name="attn_out_proj",
    )(a, w, x)


def _conv_tail(c, x, lng_ref, lnb_ref, w_ref, b_ref):
    mu = jnp.mean(c, axis=-1, keepdims=True)
    xc = c - mu
    y = xc * lax.rsqrt(jnp.mean(xc * xc, axis=-1, keepdims=True) + EPS)
    y = y * lng_ref[...] + lnb_ref[...]
    z = (y * jax.nn.sigmoid(y)).astype(BF16)
    return x + jnp.dot(z, w_ref[...], preferred_element_type=F32) + b_ref[...]


def _conv_seq_kernel(u_ref, halo_ref, x_ref, dw_ref, dwb_ref, lng_ref, lnb_ref, w_ref, b_ref,
                     o_ref, full_ref, c_ref):
    ts, d = c_ref.shape
    first = pl.program_id(1) == 0

    @pl.when(first)
    def _():
        full_ref[0:CONV_HALO, :] = jnp.zeros((CONV_HALO, d), F32)

    @pl.when(jnp.logical_not(first))
    def _():
        full_ref[0:CONV_HALO, :] = halo_ref[0]

    full_ref[CONV_HALO:CONV_HALO + ts, :] = u_ref[0]
    full_ref[CONV_HALO + ts:, :] = jnp.zeros((CONV_TAIL, d), F32)
    lead = CONV_HALO - (CONV_WIDTH - 1)

    n_a = (lead + CONV_WIDTH - 1) // CONV_ROWS + 1
    sub = lax.broadcasted_iota(jnp.int32, (CONV_ROWS, HEAD_W), 0)

    for cb in range(d // HEAD_W):
        sl = slice(cb * HEAD_W, (cb + 1) * HEAD_W)
        taps = {}
        for a in range(n_a):
            for b in range(CONV_ROWS):
                j = CONV_ROWS * a + b - lead
                if 0 <= j < CONV_WIDTH:
                    taps[a, b] = jnp.broadcast_to(dw_ref[j:j + 1, sl], (CONV_ROWS, HEAD_W))
        bias = jnp.broadcast_to(dwb_ref[:, sl], (CONV_ROWS, HEAD_W))

        def partials(r):
            base = pl.multiple_of(r * CONV_ROWS, CONV_ROWS)
            rows = [full_ref[pl.ds(base + CONV_ROWS * a, CONV_ROWS), sl] for a in range(n_a)]
            out = []
            for b in range(CONV_ROWS):
                z = None
                for a in range(n_a):
                    if (a, b) in taps:
                        term = rows[a] * taps[a, b]
                        z = term if z is None else z + term
                out.append(z)
            return tuple(out)

        def block(r, z_prev):
            z_next = partials(r)
            acc = bias + z_prev[0]
            for b in range(1, CONV_ROWS):
                mixed = jnp.where(sub >= b, z_prev[b], z_next[b])
                acc = acc + pltpu.roll(mixed, CONV_ROWS - b, 0)
            c_ref[pl.ds(pl.multiple_of((r - 1) * CONV_ROWS, CONV_ROWS), CONV_ROWS), sl] = acc
            return z_next

        lax.fori_loop(1, ts // CONV_ROWS + 1, block, partials(0), unroll=2)

    o_ref[0] = _conv_tail(c_ref[...], x_ref[0], lng_ref, lnb_ref, w_ref, b_ref)


def _conv_seq(u, x, dw, dwb, lng, lnb, w, b, l):
    bsz, s, d = u.shape
    ts = min(TOKEN_TILE, s)
    assert s % ts == 0 and ts % CONV_HALO == 0
    hb = ts // CONV_HALO
    tile = pl.BlockSpec((1, ts, d), lambda bi, i: (bi, i, 0))
    vec = _resident((None, 1, d), lambda bi, i: (l, 0, 0))
    return pl.pallas_call(
        _conv_seq_kernel,
        grid=(bsz, s // ts),
        in_specs=[
            tile,
            pl.BlockSpec((1, CONV_HALO, d), lambda bi, i: (bi, jnp.maximum(i * hb - 1, 0), 0)),
            tile,
            _resident((None, CONV_WIDTH, d), lambda bi, i: (l, 0, 0)),
            vec, vec, vec,
            _resident((None, d, d), lambda bi, i: (l, 0, 0)),
            vec,
        ],
        out_specs=tile,
        out_shape=jax.ShapeDtypeStruct((bsz, s, d), F32),
        scratch_shapes=[pltpu.VMEM((ts + CONV_HALO + CONV_TAIL, d), F32), pltpu.VMEM((ts, d), F32)],
        compiler_params=_params(2),
        name="conv_seq",
    )(u, u, x, dw, dwb, lng, lnb, w, b)


def _conv_step_kernel(st_ref, u_ref, x_ref, dw_ref, dwb_ref, lng_ref, lnb_ref, w_ref, b_ref,
                      o_ref, c_ref):
    nb = c_ref.shape[0]
    hist = CONV_WIDTH - 1
    w_hist = dw_ref[0:hist, :]
    for r in range(nb):
        c_ref[r:r + 1, :] = jnp.sum(st_ref[r] * w_hist, axis=0, keepdims=True)
    c = c_ref[...] + u_ref[...] * dw_ref[hist:hist + 1, :] + dwb_ref[...]
    o_ref[...] = _conv_tail(c, x_ref[...], lng_ref, lnb_ref, w_ref, b_ref)


def _conv_step(state, u, x, dw, dwb, lng, lnb, w, b, l):
    n, d = u.shape
    nb = min(16, n)
    assert n % nb == 0
    hist = CONV_WIDTH - 1
    tok = pl.BlockSpec((nb, d), lambda r: (r, 0))
    vec = _resident((None, 1, d), lambda r: (l, 0, 0))
    return pl.pallas_call(
        _conv_step_kernel,
        grid=(n // nb,),
        in_specs=[
            pl.BlockSpec((None, nb, hist, d), lambda r: (l, r, 0, 0)),
            tok, tok,
            _resident((None, CONV_WIDTH, d), lambda r: (l, 0, 0)),
            vec, vec, vec,
            _resident((None, d, d), lambda r: (l, 0, 0)),
            vec,
        ],
        out_specs=tok,
        out_shape=jax.ShapeDtypeStruct((n, d), F32),
        scratch_shapes=[pltpu.VMEM((nb, d), F32)],
        compiler_params=_params(1),
        name="conv_step",
    )(state, u, x, dw, dwb, lng, lnb, w, b)


def _lambda(lam_ref, lam_init):
    lv = lam_ref[...]
    a = jnp.sum(lv[0:1] * lv[1:2], axis=-1, keepdims=True)
    b = jnp.sum(lv[2:3] * lv[3:4], axis=-1, keepdims=True)
    return jnp.exp(a) - jnp.exp(b) + lam_init


def _sub_norm(o, g, lam_init):
    return o * lax.rsqrt(jnp.mean(o * o, axis=-1, keepdims=True) + EPS) * g * (1.0 - lam_init)


def _flash_kernel(lam_ref, g_ref, q_ref, k_ref, v_ref, o_ref, qq_ref, m_ref, l_ref, acc_ref,
                  *, tk, lam_init):
    tq = q_ref.shape[1]
    i = pl.program_id(2)
    q = q_ref[0]
    lane = lax.broadcasted_iota(jnp.int32, q.shape, 1)
    zero = jnp.zeros_like(q)
    qq_ref[0:tq, :] = jnp.where(lane < HEAD_DIM, q, zero)
    qq_ref[tq:, :] = jnp.where(lane >= HEAD_DIM, q, zero)
    m_ref[...] = jnp.full(m_ref.shape, NEG_BIG, F32)
    l_ref[...] = jnp.zeros(l_ref.shape, F32)
    acc_ref[...] = jnp.zeros(acc_ref.shape, F32)

    def step(j, masked):
        start = pl.multiple_of(j * tk, tk)
        kc = k_ref[0, pl.ds(start, tk), :]
        vc = v_ref[0, pl.ds(start, tk), :]
        s = lax.dot_general(qq_ref[...], kc, (((1,), (1,)), ((), ())),
                            preferred_element_type=F32)
        if masked:
            qpos = i * tq + lax.broadcasted_iota(jnp.int32, s.shape, 0) % tq
            kpos = start + lax.broadcasted_iota(jnp.int32, s.shape, 1)
            s = jnp.where(kpos <= qpos, s, NEG_BIG)
        m_old = m_ref[...]
        m_new = jnp.maximum(m_old, jnp.max(s, axis=-1, keepdims=True))
        alpha = jnp.exp(m_old - m_new)
        p = jnp.exp(s - m_new)
        l_ref[...] = alpha * l_ref[...] + jnp.sum(p, axis=-1, keepdims=True)
        acc_ref[...] = alpha * acc_ref[...] + jnp.dot(p.astype(BF16), vc,
                                                      preferred_element_type=F32)
        m_ref[...] = m_new

    n_full = (i * tq) // tk
    n_all = ((i + 1) * tq - 1) // tk + 1

    def full_body(j, carry):
        step(j, False)
        return carry

    def diag_body(j, carry):
        step(j, True)
        return carry

    lax.fori_loop(0, n_full, full_body, 0)
    lax.fori_loop(n_full, n_all, diag_body, 0)

    o = acc_ref[...] / l_ref[...]
    o = o[0:tq] - _lambda(lam_ref, lam_init) * o[tq:]
    o_ref[0] = _sub_norm(o, g_ref[...], lam_init).astype(o_ref.dtype)


def _flash_attn(q, k, v, lam, g, j, lam_init):
    bsz, s, d = q.shape
    tq = min(Q_TILE, s)
    tk = min(KV_TILE, s)
    assert s % tq == 0 and s % tk == 0 and tk % tq == 0
    kv_spec = pl.BlockSpec((1, s, HEAD_W), lambda bi, h, i: (bi, 0, h))
    q_spec = pl.BlockSpec((1, tq, HEAD_W), lambda bi, h, i: (bi, i, h))
    return pl.pallas_call(
        functools.partial(_flash_kernel, tk=tk, lam_init=lam_init),
        grid=(bsz, d // HEAD_W, s // tq),
        in_specs=[
            pl.BlockSpec((None, 4, HEAD_DIM), lambda bi, h, i: (j, 0, 0)),
            pl.BlockSpec((None, 1, HEAD_W), lambda bi, h, i: (j, 0, 0)),
            q_spec, kv_spec, kv_spec,
        ],
        out_specs=q_spec,
        out_shape=jax.ShapeDtypeStruct((bsz, s, d), BF16),
        scratch_shapes=[pltpu.VMEM((2 * tq, HEAD_W), BF16), pltpu.VMEM((2 * tq, 1), F32),
                        pltpu.VMEM((2 * tq, 1), F32), pltpu.VMEM((2 * tq, HEAD_W), F32)],
        compiler_params=_params(3),
        name="flash_diff_attn",
    )(lam, g, q, k, v)


def _decode_kernel(pt_ref, lam_ref, g_ref, q_ref, kn_ref, vn_ref, *refs, n_pages, lam_init):
    del pt_ref
    k_refs = refs[:n_pages]
    v_refs = refs[n_pages:2 * n_pages]
    o_ref = refs[2 * n_pages]
    q = q_ref[0]
    n_heads = q.shape[0]
    half_sum = (lax.broadcasted_iota(jnp.int32, (HEAD_W, HEAD_W), 0) // HEAD_DIM ==
                lax.broadcasted_iota(jnp.int32, (HEAD_W, HEAD_W), 1) // HEAD_DIM).astype(BF16)

    def scores(kblk):
        n = kblk.shape[0]
        prod = (kblk * q[None]).reshape(n * n_heads, HEAD_W).astype(BF16)
        return jnp.dot(prod, half_sum, preferred_element_type=F32).reshape(n, n_heads, HEAD_W)

    m = scores(kn_ref[...])[0]
    l = jnp.ones_like(m)
    acc_a = vn_ref[0]
    acc_b = vn_ref[0]
    for p in range(n_pages):
        s = scores(k_refs[p][0])
        m_new = jnp.maximum(m, jnp.max(s, axis=0))
        alpha = jnp.exp(m - m_new)
        pr = jnp.exp(s - m_new[None])
        pr_sw = pltpu.roll(pr.reshape(-1, HEAD_W), HEAD_DIM, 1).reshape(pr.shape)
        vp = v_refs[p][0]
        l = alpha * l + jnp.sum(pr, axis=0)
        acc_a = alpha * acc_a + jnp.sum(pr * vp, axis=0)
        acc_b = pltpu.roll(alpha, HEAD_DIM, 1) * acc_b + jnp.sum(pr_sw * vp, axis=0)
        m = m_new
    inv = 1.0 / l
    o_a = acc_a * inv
    o_b = acc_b * pltpu.roll(inv, HEAD_DIM, 1)
    lane = lax.broadcasted_iota(jnp.int32, o_a.shape, 1)
    o1 = jnp.where(lane < HEAD_DIM, o_a, o_b)
    o2 = jnp.where(lane < HEAD_DIM, o_b, o_a)
    o = o1 - _lambda(lam_ref, lam_init) * o2
    o_ref[0] = _sub_norm(o, g_ref[...], lam_init).astype(o_ref.dtype)


def _decode_attn(q, k_new, v_new, cache_k, cache_v, page_table, lam, g, j, lam_init):
    n, n_heads, _ = q.shape
    n_pages = page_table.shape[1]
    page = cache_k.shape[1]
    row = pl.BlockSpec((1, n_heads, HEAD_W), lambda b, pt: (b, 0, 0))

    def page_spec(p):
        return pl.BlockSpec((1, page, n_heads, HEAD_W), lambda b, pt: (pt[b, p], 0, 0, 0))

    grid_spec = pltpu.PrefetchScalarGridSpec(
        num_scalar_prefetch=1,
        grid=(n,),
        in_specs=[
            pl.BlockSpec((None, 4, HEAD_DIM), lambda b, pt: (j, 0, 0)),
            pl.BlockSpec((None, 1, HEAD_W), lambda b, pt: (j, 0, 0)),
            row, row, row,
        ] + [page_spec(p) for p in range(n_pages)] * 2,
        out_specs=row,
    )
    return pl.pallas_call(
        functools.partial(_decode_kernel, n_pages=n_pages, lam_init=lam_init),
        grid_spec=grid_spec,
        out_shape=jax.ShapeDtypeStruct((n, n_heads, HEAD_W), BF16),
        compiler_params=_params(1),
        name="paged_decode_attn",
    )(page_table, lam, g, q, k_new, v_new, *([cache_k] * n_pages), *([cache_v] * n_pages))


def _rope_tables(pos):
    half = HEAD_DIM // 2
    freqs = ROPE_THETA ** (-jnp.arange(half, dtype=F32) * 2.0 / HEAD_DIM)
    ang = pos.astype(F32)[:, None] * freqs[None, :]
    cos, sin = jnp.cos(ang), jnp.sin(ang)
    return jnp.tile(cos, (1, 4)), jnp.tile(jnp.concatenate([-sin, sin], axis=1), (1, 2))


def kernel(x_prompt, x_sample, state_conv, cache_k, cache_v, page_table, norm_ffn, ffn_w_in, ffn_w_out, norm_mix, conv_w_in, conv_b_in, conv_dw, conv_dw_b, conv_ln_g, conv_ln_b, conv_w_out, conv_b_out, kv_norm, w_kv, w_q, lambda_q1, lambda_k1, lambda_q2, lambda_k2, subln_g, w_o, norm_final):
    bsz, s, d = x_prompt.shape
    n_dec, dec_seq, _ = x_sample.shape
    assert dec_seq == 1
    depth = norm_mix.shape[0]
    n_a = conv_w_in.shape[0]
    n_heads = d // HEAD_W
    hist = CONV_WIDTH - 1
    past_len = page_table.shape[1] * cache_k.shape[1]

    w_in = ffn_w_in.astype(BF16)
    w_out = ffn_w_out.astype(BF16)
    cw_in = conv_w_in.astype(BF16)
    cw_out = conv_w_out.astype(BF16)
    wkv = w_kv.astype(BF16)
    wq = w_q.astype(BF16)
    wo = w_o.astype(BF16)
    g_ffn = norm_ffn[:, :, None, :]
    g_mix = norm_mix[:, None, :]
    g_kv = kv_norm[None, :]
    g_fin = norm_final[None, :]
    cb_in = conv_b_in[:, None, :]
    cdw_b = conv_dw_b[:, None, :]
    cln_g = conv_ln_g[:, None, :]
    cln_b = conv_ln_b[:, None, :]
    cb_out = conv_b_out[:, None, :]
    lam = jnp.stack([lambda_q1, lambda_k1, lambda_q2, lambda_k2], axis=1)
    g_sub = subln_g[:, None, :]

    def run(x, cos, sin, pos_tiles, conv_fn, attn_fn):
        conv_rows = []
        k = v = kv_ctx = None
        for l in range(depth):
            if l == n_a:
                k, v, kb, vb = _kv_proj(x, g_kv, wkv, cos, sin, pos_tiles)
                kv_ctx = (k, v, kb, vb)
            x = _ffn(x, g_ffn, w_in, w_out, l, 0)
            if l < n_a:
                u = _glu_proj(x, g_mix, cw_in, cb_in, l)
                x = conv_fn(u, x, l)
                conv_rows.append(u)
            else:
                j = l - n_a
                lam_init = 0.8 - 0.6 * math.exp(-0.3 * l)
                o = attn_fn(x, kv_ctx, l, j, lam_init)
                x = _out_proj(o, wo, x, j)
            x = _ffn(x, g_ffn, w_in, w_out, l, 1, g_fin if l == depth - 1 else None)
        return x, conv_rows, k, v

    cos_p, sin_p = _rope_tables(jnp.arange(s, dtype=jnp.int32))
    tiles_p = s // min(TOKEN_TILE, s)

    def conv_prompt(u, x, l):
        y = _conv_seq(u.reshape(bsz, s, d), x.reshape(bsz, s, d), conv_dw, cdw_b, cln_g, cln_b,
                      cw_out, cb_out, l)
        return y.reshape(bsz * s, d)

    def attn_prompt(x, kv_ctx, l, j, lam_init):
        q = _q_proj(x, g_mix, wq, cos_p, sin_p, tiles_p, l, j, BF16)
        _, _, kb, vb = kv_ctx
        o = _flash_attn(q.reshape(bsz, s, d), kb.reshape(bsz, s, d), vb.reshape(bsz, s, d),
                        lam, g_sub, j, lam_init)
        return o.reshape(bsz * s, d)

    y_p, conv_p, k_p, v_p = run(x_prompt.reshape(bsz * s, d), cos_p, sin_p, tiles_p,
                                conv_prompt, attn_prompt)
    conv_p = jnp.stack([u.reshape(bsz, s, d)[:, s - hist:, :] for u in conv_p], axis=0)

    cos_s, sin_s = _rope_tables(jnp.full((n_dec,), past_len, dtype=jnp.int32))

    def conv_sample(u, x, l):
        return _conv_step(state_conv, u, x, conv_dw, cdw_b, cln_g, cln_b, cw_out, cb_out, l)

    def attn_sample(x, kv_ctx, l, j, lam_init):
        q = _q_proj(x, g_mix, wq, cos_s, sin_s, 1, l, j, F32)
        k, v, _, _ = kv_ctx
        o = _decode_attn(q.reshape(n_dec, n_heads, HEAD_W), k.reshape(n_dec, n_heads, HEAD_W),
                         v.reshape(n_dec, n_heads, HEAD_W), cache_k, cache_v, page_table,
                         lam, g_sub, j, lam_init)
        return o.reshape(n_dec, d)

    y_s, conv_s, k_s, v_s = run(x_sample.reshape(n_dec, d), cos_s, sin_s, 1,
                                conv_sample, attn_sample)
    conv_s = jnp.stack([jnp.concatenate([state_conv[l][:, 1:, :], u[:, None, :]], axis=1)
                        for l, u in enumerate(conv_s)], axis=0)

    return (y_p.reshape(bsz, s, d), y_s.reshape(n_dec, 1, d), conv_p,
            k_p.reshape(bsz, s, n_heads, HEAD_W), v_p.reshape(bsz, s, n_heads, HEAD_W),
            conv_s, k_s.reshape(n_dec, 1, n_heads, HEAD_W), v_s.reshape(n_dec, 1, n_heads, HEAD_W))
```

```python
import functools
import math

import jax
import jax.numpy as jnp
from jax import lax
from jax.experimental import pallas as pl
from jax.experimental.pallas import tpu as pltpu

HEAD_DIM = 64
HEAD_W = 2 * HEAD_DIM
CONV_WIDTH = 31
CONV_HALO = 32
ROPE_THETA = 10000.0
EPS = 1e-6
NEG_BIG = -1e30

TOKEN_TILE = 512
FFN_CHUNK = 256
CONV_ROWS = 8
CONV_TAIL = CONV_ROWS
VMEM_LIMIT = 56 * 1024 * 1024

F32 = jnp.float32
BF16 = jnp.bfloat16


def _params(n_axes, vmem=VMEM_LIMIT):
    return pltpu.CompilerParams(dimension_semantics=("arbitrary",) * n_axes,
                                vmem_limit_bytes=vmem)


def _rms(x, g):
    return x * lax.rsqrt(jnp.mean(x * x, axis=-1, keepdims=True) + EPS) * g


def _resident(shape, index_map):
    return pl.BlockSpec(shape, index_map, pipeline_mode=pl.Buffered(1))


def _ffn_kernel(x_ref, g_ref, wg_ref, wu_ref, wo_ref, *rest, final_norm):
    if final_norm:
        gf_ref, o_ref, xn_ref, h_ref = rest
    else:
        o_ref, xn_ref, h_ref = rest
    x = x_ref[...]
    xn_ref[...] = _rms(x, g_ref[...]).astype(BF16)
    d_ff = h_ref.shape[1]
    for c in range(0, d_ff, FFN_CHUNK):
        xn = xn_ref[...]
        gate = jnp.dot(xn, wg_ref[:, c:c + FFN_CHUNK], preferred_element_type=F32)
        up = jnp.dot(xn, wu_ref[:, c:c + FFN_CHUNK], preferred_element_type=F32)
        h_ref[:, c:c + FFN_CHUNK] = (gate * jax.nn.sigmoid(gate) * up).astype(BF16)
    y = x + 0.5 * jnp.dot(h_ref[...], wo_ref[...], preferred_element_type=F32)
    if final_norm:
        y = _rms(y, gf_ref[...])
    o_ref[...] = y


def _ffn(x, gains, w_in, w_out, l, i, final_gain=None):
    t, d = x.shape
    f = w_out.shape[2]
    assert f % FFN_CHUNK == 0
    tm = min(TOKEN_TILE, t)
    assert t % tm == 0
    in_specs = [
        pl.BlockSpec((tm, d), lambda r: (r, 0)),
        _resident((None, None, 1, d), lambda r: (l, i, 0, 0)),
        _resident((None, None, d, f), lambda r: (l, i, 0, 0)),
        _resident((None, None, d, f), lambda r: (l, i, 0, 1)),
        _resident((None, None, f, d), lambda r: (l, i, 0, 0)),
    ]
    args = [x, gains, w_in, w_in, w_out]
    if final_gain is not None:
        in_specs.append(_resident((1, d), lambda r: (0, 0)))
        args.append(final_gain)
    return pl.pallas_call(
        functools.partial(_ffn_kernel, final_norm=final_gain is not None),
        grid=(t // tm,),
        in_specs=in_specs,
        out_specs=pl.BlockSpec((tm, d), lambda r: (r, 0)),
        out_shape=jax.ShapeDtypeStruct((t, d), F32),
        scratch_shapes=[pltpu.VMEM((tm, d), BF16), pltpu.VMEM((tm, f), BF16)],
        compiler_params=_params(1),
        name="ffn",
    )(*args)


def _glu_kernel(x_ref, g_ref, w_ref, b_ref, u_ref):
    d = u_ref.shape[1]
    xn = _rms(x_ref[...], g_ref[...]).astype(BF16)
    y = jnp.dot(xn, w_ref[...], preferred_element_type=F32) + b_ref[...]
    u_ref[...] = y[:, :d] * jax.nn.sigmoid(y[:, d:])


def _glu_proj(x, gains, w, b, l):
    t, d = x.shape
    tm = min(TOKEN_TILE, t)
    return pl.pallas_call(
        _glu_kernel,
        grid=(t // tm,),
        in_specs=[
            pl.BlockSpec((tm, d), lambda r: (r, 0)),
            _resident((None, 1, d), lambda r: (l, 0, 0)),
            _resident((None, d, 2 * d), lambda r: (l, 0, 0)),
            _resident((None, 1, 2 * d), lambda r: (l, 0, 0)),
        ],
        out_specs=pl.BlockSpec((tm, d), lambda r: (r, 0)),
        out_shape=jax.ShapeDtypeStruct((t, d), F32),
        compiler_params=_params(1),
        name="conv_glu",
    )(x, gains, w, b)


def _rope_head(xh, cos, sin_signed):
    half = HEAD_DIM // 2
    lane = lax.broadcasted_iota(jnp.int32, xh.shape, 1)
    partner = jnp.where(lane % HEAD_DIM < half,
                        pltpu.roll(xh, HEAD_W - half, 1),
                        pltpu.roll(xh, half, 1))
    return xh * cos + partner * sin_signed


def _kv_kernel(x_ref, g_ref, w_ref, cos_ref, sin_ref, k_ref, v_ref, *attn_refs):
    d = k_ref.shape[1]
    xn = _rms(x_ref[...], g_ref[...]).astype(BF16)
    y = jnp.dot(xn, w_ref[...], preferred_element_type=F32)
    v_ref[...] = y[:, d:]
    cos = cos_ref[...]
    sin = sin_ref[...]
    for h in range(d // HEAD_W):
        sl = slice(h * HEAD_W, (h + 1) * HEAD_W)
        kh = _rope_head(y[:, sl], cos, sin)
        k_ref[:, sl] = kh
        if attn_refs:
            kb_ref, vt_ref = attn_refs
            kb_ref[:, sl] = kh.astype(BF16)
            vt_ref[h] = y[:, d + h * HEAD_W:d + (h + 1) * HEAD_W].T.astype(BF16)


def _kv_proj(x, gain, w, cos, sin, pos_tiles, attn_layout):
    t, d = x.shape
    tm = min(TOKEN_TILE, t)
    tok = pl.BlockSpec((tm, d), lambda r: (r, 0))
    pos = pl.BlockSpec((tm, HEAD_W), lambda r: (r % pos_tiles, 0))
    out_specs = [tok, tok]
    out_shape = [jax.ShapeDtypeStruct((t, d), F32), jax.ShapeDtypeStruct((t, d), F32)]
    if attn_layout:
        n_heads = d // HEAD_W
        out_specs += [tok, pl.BlockSpec((None, n_heads, None, HEAD_W, tm),
                                        lambda r: (r // pos_tiles, 0, r % pos_tiles, 0, 0))]
        out_shape += [jax.ShapeDtypeStruct((t, d), BF16),
                      jax.ShapeDtypeStruct((t // (pos_tiles * tm), n_heads, pos_tiles, HEAD_W, tm),
                                           BF16)]
    return pl.pallas_call(
        _kv_kernel,
        grid=(t // tm,),
        in_specs=[tok, _resident((1, d), lambda r: (0, 0)),
                  _resident((d, 2 * d), lambda r: (0, 0)), pos, pos],
        out_specs=out_specs,
        out_shape=out_shape,
        compiler_params=_params(1),
        name="kv_proj",
    )(x, gain, w, cos, sin)


def _q_kernel(x_ref, g_ref, w_ref, cos_ref, sin_ref, q_ref):
    d = q_ref.shape[1]
    xn = _rms(x_ref[...], g_ref[...]).astype(BF16)
    y = jnp.dot(xn, w_ref[...], preferred_element_type=F32)
    cos = cos_ref[...]
    sin = sin_ref[...]
    scale = HEAD_DIM ** -0.5
    for h in range(d // HEAD_W):
        sl = slice(h * HEAD_W, (h + 1) * HEAD_W)
        q_ref[:, sl] = (_rope_head(y[:, sl], cos, sin) * scale).astype(q_ref.dtype)


def _q_proj(x, gains, w, cos, sin, pos_tiles, l, j, out_dtype):
    t, d = x.shape
    tm = min(TOKEN_TILE, t)
    tok = pl.BlockSpec((tm, d), lambda r: (r, 0))
    pos = pl.BlockSpec((tm, HEAD_W), lambda r: (r % pos_tiles, 0))
    return pl.pallas_call(
        _q_kernel,
        grid=(t // tm,),
        in_specs=[tok, _resident((None, 1, d), lambda r: (l, 0, 0)),
                  _resident((None, d, d), lambda r: (j, 0, 0)), pos, pos],
        out_specs=tok,
        out_shape=jax.ShapeDtypeStruct((t, d), out_dtype),
        compiler_params=_params(1),
        name="q_proj",
    )(x, gains, w, cos, sin)


def _out_kernel(a_ref, w_ref, x_ref, o_ref):
    o_ref[...] = x_ref[...] + jnp.dot(a_ref[...], w_ref[...], preferred_element_type=F32)


def _out_proj(a, w, x, j):
    t, d = x.shape
    tm = min(TOKEN_TILE, t)
    tok = pl.BlockSpec((tm, d), lambda r: (r, 0))
    return pl.pallas_call(
        _out_kernel,
        grid=(t // tm,),
        in_specs=[tok, _resident((None, d, d), lambda r: (j, 0, 0)), tok],
        out_specs=tok,
        out_shape=jax.ShapeDtypeStruct((t, d), F32),
        compiler_params=_params(1),
        name="attn_out_proj",
    )(a, w, x)


def _conv_tail(c, x, lng_ref, lnb_ref, w_ref, b_ref):
    mu = jnp.mean(c, axis=-1, keepdims=True)
    xc = c - mu
    y = xc * lax.rsqrt(jnp.mean(xc * xc, axis=-1, keepdims=True) + EPS)
    y = y * lng_ref[...] + lnb_ref[...]
    z = (y * jax.nn.sigmoid(y)).astype(BF16)
    return x + jnp.dot(z, w_ref[...], preferred_element_type=F32) + b_ref[...]


def _conv_seq_kernel(u_ref, halo_ref, x_ref, dw_ref, dwb_ref, lng_ref, lnb_ref, w_ref, b_ref,
                     o_ref, full_ref, c_ref):
    ts, d = c_ref.shape
    first = pl.program_id(1) == 0

    @pl.when(first)
    def _():
        full_ref[0:CONV_HALO, :] = jnp.zeros((CONV_HALO, d), F32)

    @pl.when(jnp.logical_not(first))
    def _():
        full_ref[0:CONV_HALO, :] = halo_ref[0]

    full_ref[CONV_HALO:CONV_HALO + ts, :] = u_ref[0]
    full_ref[CONV_HALO + ts:, :] = jnp.zeros((CONV_TAIL, d), F32)
    lead = CONV_HALO - (CONV_WIDTH - 1)

    n_a = (lead + CONV_WIDTH - 1) // CONV_ROWS + 1
    sub = lax.broadcasted_iota(jnp.int32, (CONV_ROWS, HEAD_W), 0)

    for cb in range(d // HEAD_W):
        sl = slice(cb * HEAD_W, (cb + 1) * HEAD_W)
        taps = {}
        for a in range(n_a):
            for b in range(CONV_ROWS):
                j = CONV_ROWS * a + b - lead
                if 0 <= j < CONV_WIDTH:
                    taps[a, b] = jnp.broadcast_to(dw_ref[j:j + 1, sl], (CONV_ROWS, HEAD_W))
        bias = jnp.broadcast_to(dwb_ref[:, sl], (CONV_ROWS, HEAD_W))

        def partials(r):
            base = pl.multiple_of(r * CONV_ROWS, CONV_ROWS)
            rows = [full_ref[pl.ds(base + CONV_ROWS * a, CONV_ROWS), sl] for a in range(n_a)]
            out = []
            for b in range(CONV_ROWS):
                z = None
                for a in range(n_a):
                    if (a, b) in taps:
                        term = rows[a] * taps[a, b]
                        z = term if z is None else z + term
                out.append(z)
            return tuple(out)

        def block(r, z_prev):
            z_next = partials(r)
            acc = bias + z_prev[0]
            for b in range(1, CONV_ROWS):
                mixed = jnp.where(sub >= b, z_prev[b], z_next[b])
                acc = acc + pltpu.roll(mixed, CONV_ROWS - b, 0)
            c_ref[pl.ds(pl.multiple_of((r - 1) * CONV_ROWS, CONV_ROWS), CONV_ROWS), sl] = acc
            return z_next

        lax.fori_loop(1, ts // CONV_ROWS + 1, block, partials(0), unroll=2)

    o_ref[0] = _conv_tail(c_ref[...], x_ref[0], lng_ref, lnb_ref, w_ref, b_ref)


def _conv_seq(u, x, dw, dwb, lng, lnb, w, b, l):
    bsz, s, d = u.shape
    ts = min(TOKEN_TILE, s)
    assert s % ts == 0 and ts % CONV_HALO == 0
    hb = ts // CONV_HALO
    tile = pl.BlockSpec((1, ts, d), lambda bi, i: (bi, i, 0))
    vec = _resident((None, 1, d), lambda bi, i: (l, 0, 0))
    return pl.pallas_call(
        _conv_seq_kernel,
        grid=(bsz, s // ts),
        in_specs=[
            tile,
            pl.BlockSpec((1, CONV_HALO, d), lambda bi, i: (bi, jnp.maximum(i * hb - 1, 0), 0)),
            tile,
            _resident((None, CONV_WIDTH, d), lambda bi, i: (l, 0, 0)),
            vec, vec, vec,
            _resident((None, d, d), lambda bi, i: (l, 0, 0)),
            vec,
        ],
        out_specs=tile,
        out_shape=jax.ShapeDtypeStruct((bsz, s, d), F32),
        scratch_shapes=[pltpu.VMEM((ts + CONV_HALO + CONV_TAIL, d), F32), pltpu.VMEM((ts, d), F32)],
        compiler_params=_params(2),
        name="conv_seq",
    )(u, u, x, dw, dwb, lng, lnb, w, b)


def _conv_step_kernel(st_ref, u_ref, x_ref, dw_ref, dwb_ref, lng_ref, lnb_ref, w_ref, b_ref,
                      o_ref, c_ref):
    nb = c_ref.shape[0]
    hist = CONV_WIDTH - 1
    w_hist = dw_ref[0:hist, :]
    for r in range(nb):
        c_ref[r:r + 1, :] = jnp.sum(st_ref[r] * w_hist, axis=0, keepdims=True)
    c = c_ref[...] + u_ref[...] * dw_ref[hist:hist + 1, :] + dwb_ref[...]
    o_ref[...] = _conv_tail(c, x_ref[...], lng_ref, lnb_ref, w_ref, b_ref)


def _conv_step(state, u, x, dw, dwb, lng, lnb, w, b, l):
    n, d = u.shape
    nb = min(16, n)
    assert n % nb == 0
    hist = CONV_WIDTH - 1
    tok = pl.BlockSpec((nb, d), lambda r: (r, 0))
    vec = _resident((None, 1, d), lambda r: (l, 0, 0))
    return pl.pallas_call(
        _conv_step_kernel,
        grid=(n // nb,),
        in_specs=[
            pl.BlockSpec((None, nb, hist, d), lambda r: (l, r, 0, 0)),
            tok, tok,
            _resident((None, CONV_WIDTH, d), lambda r: (l, 0, 0)),
            vec, vec, vec,
            _resident((None, d, d), lambda r: (l, 0, 0)),
            vec,
        ],
        out_specs=tok,
        out_shape=jax.ShapeDtypeStruct((n, d), F32),
        scratch_shapes=[pltpu.VMEM((nb, d), F32)],
        compiler_params=_params(1),
        name="conv_step",
    )(state, u, x, dw, dwb, lng, lnb, w, b)


def _lambda(lam_ref, lam_init):
    lv = lam_ref[...]
    a = jnp.sum(lv[0:1] * lv[1:2], axis=-1, keepdims=True)
    b = jnp.sum(lv[2:3] * lv[3:4], axis=-1, keepdims=True)
    return jnp.exp(a) - jnp.exp(b) + lam_init


def _sub_norm(o, g, lam_init):
    return o * lax.rsqrt(jnp.mean(o * o, axis=-1, keepdims=True) + EPS) * g * (1.0 - lam_init)


def _flash_kernel(lam_ref, g_ref, q_ref, k_ref, vt_ref, o_ref, qq_ref, acc_ref, *, lam_init):
    tq = q_ref.shape[1]
    tk = vt_ref.shape[2]
    i = pl.program_id(2)
    q = q_ref[0]
    lane = lax.broadcasted_iota(jnp.int32, q.shape, 1)
    zero = jnp.zeros_like(q)
    qq_ref[0] = jnp.where(lane < HEAD_DIM, q, zero)
    qq_ref[1] = jnp.where(lane >= HEAD_DIM, q, zero)
    acc_ref[...] = jnp.zeros(acc_ref.shape, F32)

    def step(j, carry, masked):
        kc = k_ref[0, pl.ds(pl.multiple_of(j * tk, tk), tk), :]
        vt = vt_ref[j]
        out = []
        for c in range(2):
            m_old, l_old = carry[c]
            s = lax.dot_general(kc, qq_ref[c], (((1,), (1,)), ((), ())),
                                preferred_element_type=F32)
            if masked:
                kpos = j * tk + lax.broadcasted_iota(jnp.int32, s.shape, 0)
                qpos = i * tq + lax.broadcasted_iota(jnp.int32, s.shape, 1)
                s = jnp.where(kpos <= qpos, s, NEG_BIG)
            m_new = jnp.maximum(m_old, jnp.max(s, axis=0, keepdims=True))
            alpha = jnp.exp(m_old - m_new)
            p = jnp.exp(s - m_new)
            l_new = alpha * l_old + jnp.sum(p, axis=0, keepdims=True)
            acc_ref[c] = alpha * acc_ref[c] + jnp.dot(vt, p.astype(BF16),
                                                      preferred_element_type=F32)
            out.append((m_new, l_new))
        return tuple(out)

    stat = (jnp.full((1, tq), NEG_BIG, F32), jnp.zeros((1, tq), F32))
    carry = lax.fori_loop(0, i, lambda j, c: step(j, c, False), (stat, stat))
    (_, l1), (_, l2) = step(i, carry, True)

    ot = acc_ref[0] * (1.0 / l1) - _lambda(lam_ref, lam_init) * (acc_ref[1] * (1.0 / l2))
    ot = ot * lax.rsqrt(jnp.mean(ot * ot, axis=0, keepdims=True) + EPS)
    ot = ot * (g_ref[...] * (1.0 - lam_init))
    o_ref[0] = ot.T.astype(o_ref.dtype)


def _flash_attn(q, k, vt, lam, g_col, j, lam_init):
    bsz, s, d = q.shape
    tile = vt.shape[4]
    assert s % tile == 0 and vt.shape[2] == s // tile
    q_spec = pl.BlockSpec((1, tile, HEAD_W), lambda bi, h, i: (bi, i, h))
    return pl.pallas_call(
        functools.partial(_flash_kernel, lam_init=lam_init),
        grid=(bsz, d // HEAD_W, s // tile),
        in_specs=[
            pl.BlockSpec((None, 4, HEAD_DIM), lambda bi, h, i: (j, 0, 0)),
            pl.BlockSpec((None, HEAD_W, 1), lambda bi, h, i: (j, 0, 0)),
            q_spec,
            pl.BlockSpec((1, s, HEAD_W), lambda bi, h, i: (bi, 0, h)),
            pl.BlockSpec((None, None, s // tile, HEAD_W, tile), lambda bi, h, i: (bi, h, 0, 0, 0)),
        ],
        out_specs=q_spec,
        out_shape=jax.ShapeDtypeStruct((bsz, s, d), BF16),
        scratch_shapes=[pltpu.VMEM((2, tile, HEAD_W), BF16), pltpu.VMEM((2, HEAD_W, tile), F32)],
        compiler_params=_params(3),
        name="flash_diff_attn",
    )(lam, g_col, q, k, vt)


def _decode_kernel(pt_ref, lam_ref, g_ref, q_ref, kn_ref, vn_ref, *refs, n_pages, lam_init):
    del pt_ref
    k_refs = refs[:n_pages]
    v_refs = refs[n_pages:2 * n_pages]
    o_ref = refs[2 * n_pages]
    q = q_ref[0]
    n_heads = q.shape[0]
    half_sum = (lax.broadcasted_iota(jnp.int32, (HEAD_W, HEAD_W), 0) // HEAD_DIM ==
                lax.broadcasted_iota(jnp.int32, (HEAD_W, HEAD_W), 1) // HEAD_DIM).astype(BF16)

    def scores(kblk):
        n = kblk.shape[0]
        prod = (kblk * q[None]).reshape(n * n_heads, HEAD_W).astype(BF16)
        return jnp.dot(prod, half_sum, preferred_element_type=F32).reshape(n, n_heads, HEAD_W)

    m = scores(kn_ref[...])[0]
    l = jnp.ones_like(m)
    acc_a = vn_ref[0]
    acc_b = vn_ref[0]
    for p in range(n_pages):
        s = scores(k_refs[p][0])
        m_new = jnp.maximum(m, jnp.max(s, axis=0))
        alpha = jnp.exp(m - m_new)
        pr = jnp.exp(s - m_new[None])
        pr_sw = pltpu.roll(pr.reshape(-1, HEAD_W), HEAD_DIM, 1).reshape(pr.shape)
        vp = v_refs[p][0]
        l = alpha * l + jnp.sum(pr, axis=0)
        acc_a = alpha * acc_a + jnp.sum(pr * vp, axis=0)
        acc_b = pltpu.roll(alpha, HEAD_DIM, 1) * acc_b + jnp.sum(pr_sw * vp, axis=0)
        m = m_new
    inv = 1.0 / l
    o_a = acc_a * inv
    o_b = acc_b * pltpu.roll(inv, HEAD_DIM, 1)
    lane = lax.broadcasted_iota(jnp.int32, o_a.shape, 1)
    o1 = jnp.where(lane < HEAD_DIM, o_a, o_b)
    o2 = jnp.where(lane < HEAD_DIM, o_b, o_a)
    o = o1 - _lambda(lam_ref, lam_init) * o2
    o_ref[0] = _sub_norm(o, g_ref[...], lam_init).astype(o_ref.dtype)


def _decode_attn(q, k_new, v_new, cache_k, cache_v, page_table, lam, g, j, lam_init):
    n, n_heads, _ = q.shape
    n_pages = page_table.shape[1]
    page = cache_k.shape[1]
    row = pl.BlockSpec((1, n_heads, HEAD_W), lambda b, pt: (b, 0, 0))

    def page_spec(p):
        return pl.BlockSpec((1, page, n_heads, HEAD_W), lambda b, pt: (pt[b, p], 0, 0, 0))

    grid_spec = pltpu.PrefetchScalarGridSpec(
        num_scalar_prefetch=1,
        grid=(n,),
        in_specs=[
            pl.BlockSpec((None, 4, HEAD_DIM), lambda b, pt: (j, 0, 0)),
            pl.BlockSpec((None, 1, HEAD_W), lambda b, pt: (j, 0, 0)),
            row, row, row,
        ] + [page_spec(p) for p in range(n_pages)] * 2,
        out_specs=row,
    )
    return pl.pallas_call(
        functools.partial(_decode_kernel, n_pages=n_pages, lam_init=lam_init),
        grid_spec=grid_spec,
        out_shape=jax.ShapeDtypeStruct((n, n_heads, HEAD_W), BF16),
        compiler_params=_params(1),
        name="paged_decode_attn",
    )(page_table, lam, g, q, k_new, v_new, *([cache_k] * n_pages), *([cache_v] * n_pages))


def _rope_tables(pos):
    half = HEAD_DIM // 2
    freqs = ROPE_THETA ** (-jnp.arange(half, dtype=F32) * 2.0 / HEAD_DIM)
    ang = pos.astype(F32)[:, None] * freqs[None, :]
    cos, sin = jnp.cos(ang), jnp.sin(ang)
    return jnp.tile(cos, (1, 4)), jnp.tile(jnp.concatenate([-sin, sin], axis=1), (1, 2))


def kernel(x_prompt, x_sample, state_conv, cache_k, cache_v, page_table, norm_ffn, ffn_w_in, ffn_w_out, norm_mix, conv_w_in, conv_b_in, conv_dw, conv_dw_b, conv_ln_g, conv_ln_b, conv_w_out, conv_b_out, kv_norm, w_kv, w_q, lambda_q1, lambda_k1, lambda_q2, lambda_k2, subln_g, w_o, norm_final):
    bsz, s, d = x_prompt.shape
    n_dec, dec_seq, _ = x_sample.shape
    assert dec_seq == 1
    depth = norm_mix.shape[0]
    n_a = conv_w_in.shape[0]
    n_heads = d // HEAD_W
    hist = CONV_WIDTH - 1
    past_len = page_table.shape[1] * cache_k.shape[1]

    w_in = ffn_w_in.astype(BF16)
    w_out = ffn_w_out.astype(BF16)
    cw_in = conv_w_in.astype(BF16)
    cw_out = conv_w_out.astype(BF16)
    wkv = w_kv.astype(BF16)
    wq = w_q.astype(BF16)
    wo = w_o.astype(BF16)
    g_ffn = norm_ffn[:, :, None, :]
    g_mix = norm_mix[:, None, :]
    g_kv = kv_norm[None, :]
    g_fin = norm_final[None, :]
    cb_in = conv_b_in[:, None, :]
    cdw_b = conv_dw_b[:, None, :]
    cln_g = conv_ln_g[:, None, :]
    cln_b = conv_ln_b[:, None, :]
    cb_out = conv_b_out[:, None, :]
    lam = jnp.stack([lambda_q1, lambda_k1, lambda_q2, lambda_k2], axis=1)
    g_sub = subln_g[:, None, :]
    g_sub_col = subln_g[:, :, None]

    def run(x, cos, sin, pos_tiles, conv_fn, attn_fn, attn_layout):
        conv_rows = []
        k = v = kv_ctx = None
        for l in range(depth):
            if l == n_a:
                kv_ctx = _kv_proj(x, g_kv, wkv, cos, sin, pos_tiles, attn_layout)
                k, v = kv_ctx[:2]
            x = _ffn(x, g_ffn, w_in, w_out, l, 0)
            if l < n_a:
                u = _glu_proj(x, g_mix, cw_in, cb_in, l)
                x = conv_fn(u, x, l)
                conv_rows.append(u)
            else:
                j = l - n_a
                lam_init = 0.8 - 0.6 * math.exp(-0.3 * l)
                o = attn_fn(x, kv_ctx, l, j, lam_init)
                x = _out_proj(o, wo, x, j)
            x = _ffn(x, g_ffn, w_in, w_out, l, 1, g_fin if l == depth - 1 else None)
        return x, conv_rows, k, v

    cos_p, sin_p = _rope_tables(jnp.arange(s, dtype=jnp.int32))
    tiles_p = s // min(TOKEN_TILE, s)

    def conv_prompt(u, x, l):
        y = _conv_seq(u.reshape(bsz, s, d), x.reshape(bsz, s, d), conv_dw, cdw_b, cln_g, cln_b,
                      cw_out, cb_out, l)
        return y.reshape(bsz * s, d)

    def attn_prompt(x, kv_ctx, l, j, lam_init):
        q = _q_proj(x, g_mix, wq, cos_p, sin_p, tiles_p, l, j, BF16)
        _, _, kb, vt = kv_ctx
        o = _flash_attn(q.reshape(bsz, s, d), kb.reshape(bsz, s, d), vt, lam, g_sub_col, j,
                        lam_init)
        return o.reshape(bsz * s, d)

    y_p, conv_p, k_p, v_p = run(x_prompt.reshape(bsz * s, d), cos_p, sin_p, tiles_p,
                                conv_prompt, attn_prompt, True)
    conv_p = jnp.stack([u.reshape(bsz, s, d)[:, s - hist:, :] for u in conv_p], axis=0)

    cos_s, sin_s = _rope_tables(jnp.full((n_dec,), past_len, dtype=jnp.int32))

    def conv_sample(u, x, l):
        return _conv_step(state_conv, u, x, conv_dw, cdw_b, cln_g, cln_b, cw_out, cb_out, l)

    def attn_sample(x, kv_ctx, l, j, lam_init):
        q = _q_proj(x, g_mix, wq, cos_s, sin_s, 1, l, j, F32)
        k, v = kv_ctx
        o = _decode_attn(q.reshape(n_dec, n_heads, HEAD_W), k.reshape(n_dec, n_heads, HEAD_W),
                         v.reshape(n_dec, n_heads, HEAD_W), cache_k, cache_v, page_table,
                         lam, g_sub, j, lam_init)
        return o.reshape(n_dec, d)

    y_s, conv_s, k_s, v_s = run(x_sample.reshape(n_dec, d), cos_s, sin_s, 1,
                                conv_sample, attn_sample, False)
    conv_s = jnp.stack([jnp.concatenate([state_conv[l][:, 1:, :], u[:, None, :]], axis=1)
                        for l, u in enumerate(conv_s)], axis=0)

    return (y_p.reshape(bsz, s, d), y_s.reshape(n_dec, 1, d), conv_p,
            k_p.reshape(bsz, s, n_heads, HEAD_W), v_p.reshape(bsz, s, n_heads, HEAD_W),
            conv_s, k_s.reshape(n_dec, 1, n_heads, HEAD_W), v_s.reshape(n_dec, 1, n_heads, HEAD_W))
```

```python
import functools
import math

import jax
import jax.numpy as jnp
from jax import lax
from jax.experimental import pallas as pl
from jax.experimental.pallas import tpu as pltpu

HEAD_DIM = 64
HEAD_W = 2 * HEAD_DIM
CONV_WIDTH = 31
CONV_HALO = 32
ROPE_THETA = 10000.0
EPS = 1e-6
NEG_BIG = -1e30

TOKEN_TILE = 512
FFN_CHUNK = 256
CONV_ROWS = 8
CONV_TAIL = CONV_ROWS
VMEM_LIMIT = 56 * 1024 * 1024

F32 = jnp.float32
BF16 = jnp.bfloat16


def _params(n_axes, vmem=VMEM_LIMIT):
    return pltpu.CompilerParams(dimension_semantics=("arbitrary",) * n_axes,
                                vmem_limit_bytes=vmem)


def _rms(x, g):
    return x * lax.rsqrt(jnp.mean(x * x, axis=-1, keepdims=True) + EPS) * g


def _resident(shape, index_map):
    return pl.BlockSpec(shape, index_map, pipeline_mode=pl.Buffered(1))


def _compact_pages(page_refs, out_ref, n_pages):
    for idx, ref in enumerate(page_refs):
        sidx, p = divmod(idx, n_pages)
        rows = ref.shape[1]
        out_ref[sidx, p * rows:(p + 1) * rows, :] = ref[0].astype(BF16)


def _ffn_kernel(*refs, final_norm, n_page_refs, n_pages):
    if n_page_refs:
        refs = refs[1:]
    x_ref, g_ref, wg_ref, wu_ref, wo_ref = refs[:5]
    refs = refs[5:]
    if final_norm:
        gf_ref = refs[0]
        refs = refs[1:]
    page_refs = refs[:n_page_refs]
    refs = refs[n_page_refs:]
    if n_page_refs:
        o_ref, slab_ref, xn_ref, h_ref = refs
        _compact_pages(page_refs, slab_ref, n_pages)
    else:
        o_ref, xn_ref, h_ref = refs
    x = x_ref[...]
    xn_ref[...] = _rms(x, g_ref[...]).astype(BF16)
    d_ff = h_ref.shape[1]
    for c in range(0, d_ff, FFN_CHUNK):
        xn = xn_ref[...]
        gate = jnp.dot(xn, wg_ref[:, c:c + FFN_CHUNK], preferred_element_type=F32)
        up = jnp.dot(xn, wu_ref[:, c:c + FFN_CHUNK], preferred_element_type=F32)
        h_ref[:, c:c + FFN_CHUNK] = (gate * jax.nn.sigmoid(gate) * up).astype(BF16)
    y = x + 0.5 * jnp.dot(h_ref[...], wo_ref[...], preferred_element_type=F32)
    if final_norm:
        y = _rms(y, gf_ref[...])
    o_ref[...] = y


def _ffn(x, gains, w_in, w_out, l, i, final_gain=None, pages=None):
    t, d = x.shape
    f = w_out.shape[2]
    assert f % FFN_CHUNK == 0
    tm = min(TOKEN_TILE, t)
    assert t % tm == 0
    steps = t // tm
    in_specs = [
        pl.BlockSpec((tm, d), lambda r, *_: (r, 0)),
        _resident((None, None, 1, d), lambda r, *_: (l, i, 0, 0)),
        _resident((None, None, d, f), lambda r, *_: (l, i, 0, 0)),
        _resident((None, None, d, f), lambda r, *_: (l, i, 0, 1)),
        _resident((None, None, f, d), lambda r, *_: (l, i, 0, 0)),
    ]
    args = [x, gains, w_in, w_in, w_out]
    if final_gain is not None:
        in_specs.append(_resident((1, d), lambda r, *_: (0, 0)))
        args.append(final_gain)
    out_specs = pl.BlockSpec((tm, d), lambda r, *_: (r, 0))
    out_shape = jax.ShapeDtypeStruct((t, d), F32)
    n_page_refs = n_pages = 0
    if pages is not None:
        page_table, cache, first_seq, n_seq = pages
        n_pages = page_table.shape[1]
        rows = cache.shape[1]
        assert n_seq % steps == 0
        per_step = n_seq // steps
        for sq in range(per_step):
            for p in range(n_pages):
                in_specs.append(pl.BlockSpec(
                    (1, rows, HEAD_W),
                    lambda r, pt, sq=sq, p=p: (pt[first_seq + r * per_step + sq, p], 0, 0)))
                args.append(cache)
        n_page_refs = per_step * n_pages
        out_specs = [out_specs,
                     pl.BlockSpec((per_step, n_pages * rows, HEAD_W), lambda r, pt: (r, 0, 0))]
        out_shape = [out_shape, jax.ShapeDtypeStruct((n_seq, n_pages * rows, HEAD_W), BF16)]
        args = [page_table] + args
    grid_spec = pltpu.PrefetchScalarGridSpec(
        num_scalar_prefetch=0 if pages is None else 1,
        grid=(steps,),
        in_specs=in_specs,
        out_specs=out_specs,
        scratch_shapes=[pltpu.VMEM((tm, d), BF16), pltpu.VMEM((tm, f), BF16)],
    )
    return pl.pallas_call(
        functools.partial(_ffn_kernel, final_norm=final_gain is not None,
                          n_page_refs=n_page_refs, n_pages=n_pages),
        grid_spec=grid_spec,
        out_shape=out_shape,
        compiler_params=_params(1),
        name="ffn",
    )(*args)


def _glu_kernel(x_ref, g_ref, w_ref, b_ref, u_ref):
    d = u_ref.shape[1]
    xn = _rms(x_ref[...], g_ref[...]).astype(BF16)
    y = jnp.dot(xn, w_ref[...], preferred_element_type=F32) + b_ref[...]
    u_ref[...] = y[:, :d] * jax.nn.sigmoid(y[:, d:])


def _glu_proj(x, gains, w, b, l):
    t, d = x.shape
    tm = min(TOKEN_TILE, t)
    return pl.pallas_call(
        _glu_kernel,
        grid=(t // tm,),
        in_specs=[
            pl.BlockSpec((tm, d), lambda r: (r, 0)),
            _resident((None, 1, d), lambda r: (l, 0, 0)),
            _resident((None, d, 2 * d), lambda r: (l, 0, 0)),
            _resident((None, 1, 2 * d), lambda r: (l, 0, 0)),
        ],
        out_specs=pl.BlockSpec((tm, d), lambda r: (r, 0)),
        out_shape=jax.ShapeDtypeStruct((t, d), F32),
        compiler_params=_params(1),
        name="conv_glu",
    )(x, gains, w, b)


def _rope_head(xh, cos, sin_signed):
    half = HEAD_DIM // 2
    lane = lax.broadcasted_iota(jnp.int32, xh.shape, 1)
    partner = jnp.where(lane % HEAD_DIM < half,
                        pltpu.roll(xh, HEAD_W - half, 1),
                        pltpu.roll(xh, half, 1))
    return xh * cos + partner * sin_signed


def _kv_kernel(x_ref, g_ref, w_ref, cos_ref, sin_ref, k_ref, v_ref, *attn_refs):
    d = k_ref.shape[1]
    xn = _rms(x_ref[...], g_ref[...]).astype(BF16)
    y = jnp.dot(xn, w_ref[...], preferred_element_type=F32)
    v_ref[...] = y[:, d:]
    cos = cos_ref[...]
    sin = sin_ref[...]
    for h in range(d // HEAD_W):
        sl = slice(h * HEAD_W, (h + 1) * HEAD_W)
        kh = _rope_head(y[:, sl], cos, sin)
        k_ref[:, sl] = kh
        if attn_refs:
            kb_ref, vt_ref = attn_refs
            kb_ref[:, sl] = kh.astype(BF16)
            vt_ref[h] = y[:, d + h * HEAD_W:d + (h + 1) * HEAD_W].T.astype(BF16)


def _kv_proj(x, gain, w, cos, sin, pos_tiles, attn_layout):
    t, d = x.shape
    tm = min(TOKEN_TILE, t)
    tok = pl.BlockSpec((tm, d), lambda r: (r, 0))
    pos = pl.BlockSpec((tm, HEAD_W), lambda r: (r % pos_tiles, 0))
    out_specs = [tok, tok]
    out_shape = [jax.ShapeDtypeStruct((t, d), F32), jax.ShapeDtypeStruct((t, d), F32)]
    if attn_layout:
        n_heads = d // HEAD_W
        out_specs += [tok, pl.BlockSpec((None, n_heads, None, HEAD_W, tm),
                                        lambda r: (r // pos_tiles, 0, r % pos_tiles, 0, 0))]
        out_shape += [jax.ShapeDtypeStruct((t, d), BF16),
                      jax.ShapeDtypeStruct((t // (pos_tiles * tm), n_heads, pos_tiles, HEAD_W, tm),
                                           BF16)]
    return pl.pallas_call(
        _kv_kernel,
        grid=(t // tm,),
        in_specs=[tok, _resident((1, d), lambda r: (0, 0)),
                  _resident((d, 2 * d), lambda r: (0, 0)), pos, pos],
        out_specs=out_specs,
        out_shape=out_shape,
        compiler_params=_params(1),
        name="kv_proj",
    )(x, gain, w, cos, sin)


def _q_kernel(x_ref, g_ref, w_ref, cos_ref, sin_ref, q_ref, *, scale):
    d = q_ref.shape[1]
    xn = _rms(x_ref[...], g_ref[...]).astype(BF16)
    y = jnp.dot(xn, w_ref[...], preferred_element_type=F32)
    cos = cos_ref[...]
    sin = sin_ref[...]
    for h in range(d // HEAD_W):
        sl = slice(h * HEAD_W, (h + 1) * HEAD_W)
        q_ref[:, sl] = (_rope_head(y[:, sl], cos, sin) * scale).astype(q_ref.dtype)


def _q_proj(x, gains, w, cos, sin, pos_tiles, l, j, scale):
    t, d = x.shape
    tm = min(TOKEN_TILE, t)
    tok = pl.BlockSpec((tm, d), lambda r: (r, 0))
    pos = pl.BlockSpec((tm, HEAD_W), lambda r: (r % pos_tiles, 0))
    return pl.pallas_call(
        functools.partial(_q_kernel, scale=scale),
        grid=(t // tm,),
        in_specs=[tok, _resident((None, 1, d), lambda r: (l, 0, 0)),
                  _resident((None, d, d), lambda r: (j, 0, 0)), pos, pos],
        out_specs=tok,
        out_shape=jax.ShapeDtypeStruct((t, d), BF16),
        compiler_params=_params(1),
        name="q_proj",
    )(x, gains, w, cos, sin)


def _out_kernel(a_ref, w_ref, x_ref, o_ref):
    o_ref[...] = x_ref[...] + jnp.dot(a_ref[...], w_ref[...], preferred_element_type=F32)


def _out_proj(a, w, x, j):
    t, d = x.shape
    tm = min(TOKEN_TILE, t)
    tok = pl.BlockSpec((tm, d), lambda r: (r, 0))
    return pl.pallas_call(
        _out_kernel,
        grid=(t // tm,),
        in_specs=[tok, _resident((None, d, d), lambda r: (j, 0, 0)), tok],
        out_specs=tok,
        out_shape=jax.ShapeDtypeStruct((t, d), F32),
        compiler_params=_params(1),
        name="attn_out_proj",
    )(a, w, x)


def _conv_tail(c, x, lng_ref, lnb_ref, w_ref, b_ref):
    mu = jnp.mean(c, axis=-1, keepdims=True)
    xc = c - mu
    y = xc * lax.rsqrt(jnp.mean(xc * xc, axis=-1, keepdims=True) + EPS)
    y = y * lng_ref[...] + lnb_ref[...]
    z = (y * jax.nn.sigmoid(y)).astype(BF16)
    return x + jnp.dot(z, w_ref[...], preferred_element_type=F32) + b_ref[...]


def _conv_seq_kernel(u_ref, halo_ref, x_ref, dw_ref, dwb_ref, lng_ref, lnb_ref, w_ref, b_ref,
                     o_ref, full_ref, c_ref):
    ts, d = c_ref.shape
    first = pl.program_id(1) == 0

    @pl.when(first)
    def _():
        full_ref[0:CONV_HALO, :] = jnp.zeros((CONV_HALO, d), F32)

    @pl.when(jnp.logical_not(first))
    def _():
        full_ref[0:CONV_HALO, :] = halo_ref[0]

    full_ref[CONV_HALO:CONV_HALO + ts, :] = u_ref[0]
    full_ref[CONV_HALO + ts:, :] = jnp.zeros((CONV_TAIL, d), F32)
    lead = CONV_HALO - (CONV_WIDTH - 1)

    n_a = (lead + CONV_WIDTH - 1) // CONV_ROWS + 1
    sub = lax.broadcasted_iota(jnp.int32, (CONV_ROWS, HEAD_W), 0)

    for cb in range(d // HEAD_W):
        sl = slice(cb * HEAD_W, (cb + 1) * HEAD_W)
        taps = {}
        for a in range(n_a):
            for b in range(CONV_ROWS):
                j = CONV_ROWS * a + b - lead
                if 0 <= j < CONV_WIDTH:
                    taps[a, b] = jnp.broadcast_to(dw_ref[j:j + 1, sl], (CONV_ROWS, HEAD_W))
        bias = jnp.broadcast_to(dwb_ref[:, sl], (CONV_ROWS, HEAD_W))

        def partials(r):
            base = pl.multiple_of(r * CONV_ROWS, CONV_ROWS)
            rows = [full_ref[pl.ds(base + CONV_ROWS * a, CONV_ROWS), sl] for a in range(n_a)]
            out = []
            for b in range(CONV_ROWS):
                z = None
                for a in range(n_a):
                    if (a, b) in taps:
                        term = rows[a] * taps[a, b]
                        z = term if z is None else z + term
                out.append(z)
            return tuple(out)

        def block(r, z_prev):
            z_next = partials(r)
            acc = bias + z_prev[0]
            for b in range(1, CONV_ROWS):
                mixed = jnp.where(sub >= b, z_prev[b], z_next[b])
                acc = acc + pltpu.roll(mixed, CONV_ROWS - b, 0)
            c_ref[pl.ds(pl.multiple_of((r - 1) * CONV_ROWS, CONV_ROWS), CONV_ROWS), sl] = acc
            return z_next

        lax.fori_loop(1, ts // CONV_ROWS + 1, block, partials(0), unroll=2)

    o_ref[0] = _conv_tail(c_ref[...], x_ref[0], lng_ref, lnb_ref, w_ref, b_ref)


def _conv_seq(u, x, dw, dwb, lng, lnb, w, b, l):
    bsz, s, d = u.shape
    ts = min(TOKEN_TILE, s)
    assert s % ts == 0 and ts % CONV_HALO == 0
    hb = ts // CONV_HALO
    tile = pl.BlockSpec((1, ts, d), lambda bi, i: (bi, i, 0))
    vec = _resident((None, 1, d), lambda bi, i: (l, 0, 0))
    return pl.pallas_call(
        _conv_seq_kernel,
        grid=(bsz, s // ts),
        in_specs=[
            tile,
            pl.BlockSpec((1, CONV_HALO, d), lambda bi, i: (bi, jnp.maximum(i * hb - 1, 0), 0)),
            tile,
            _resident((None, CONV_WIDTH, d), lambda bi, i: (l, 0, 0)),
            vec, vec, vec,
            _resident((None, d, d), lambda bi, i: (l, 0, 0)),
            vec,
        ],
        out_specs=tile,
        out_shape=jax.ShapeDtypeStruct((bsz, s, d), F32),
        scratch_shapes=[pltpu.VMEM((ts + CONV_HALO + CONV_TAIL, d), F32), pltpu.VMEM((ts, d), F32)],
        compiler_params=_params(2),
        name="conv_seq",
    )(u, u, x, dw, dwb, lng, lnb, w, b)


def _conv_step_kernel(st_ref, u_ref, x_ref, dw_ref, dwb_ref, lng_ref, lnb_ref, w_ref, b_ref,
                      o_ref, c_ref):
    nb = c_ref.shape[0]
    hist = CONV_WIDTH - 1
    w_hist = dw_ref[0:hist, :]
    for r in range(nb):
        c_ref[r:r + 1, :] = jnp.sum(st_ref[r] * w_hist, axis=0, keepdims=True)
    c = c_ref[...] + u_ref[...] * dw_ref[hist:hist + 1, :] + dwb_ref[...]
    o_ref[...] = _conv_tail(c, x_ref[...], lng_ref, lnb_ref, w_ref, b_ref)


def _conv_step(state, u, x, dw, dwb, lng, lnb, w, b, l):
    n, d = u.shape
    nb = min(16, n)
    assert n % nb == 0
    hist = CONV_WIDTH - 1
    tok = pl.BlockSpec((nb, d), lambda r: (r, 0))
    vec = _resident((None, 1, d), lambda r: (l, 0, 0))
    return pl.pallas_call(
        _conv_step_kernel,
        grid=(n // nb,),
        in_specs=[
            pl.BlockSpec((None, nb, hist, d), lambda r: (l, r, 0, 0)),
            tok, tok,
            _resident((None, CONV_WIDTH, d), lambda r: (l, 0, 0)),
            vec, vec, vec,
            _resident((None, d, d), lambda r: (l, 0, 0)),
            vec,
        ],
        out_specs=tok,
        out_shape=jax.ShapeDtypeStruct((n, d), F32),
        scratch_shapes=[pltpu.VMEM((nb, d), F32)],
        compiler_params=_params(1),
        name="conv_step",
    )(state, u, x, dw, dwb, lng, lnb, w, b)


def _lambda(lam_ref, lam_init):
    lv = lam_ref[...]
    a = jnp.sum(lv[0:1] * lv[1:2], axis=-1, keepdims=True)
    b = jnp.sum(lv[2:3] * lv[3:4], axis=-1, keepdims=True)
    return jnp.exp(a) - jnp.exp(b) + lam_init


def _sub_norm(o, g, lam_init):
    return o * lax.rsqrt(jnp.mean(o * o, axis=-1, keepdims=True) + EPS) * g * (1.0 - lam_init)


def _flash_kernel(lam_ref, g_ref, q_ref, k_ref, vt_ref, o_ref, qq_ref, acc_ref, *, lam_init):
    tq = q_ref.shape[1]
    tk = vt_ref.shape[2]
    i = pl.program_id(2)
    q = q_ref[0]
    lane = lax.broadcasted_iota(jnp.int32, q.shape, 1)
    zero = jnp.zeros_like(q)
    qq_ref[0] = jnp.where(lane < HEAD_DIM, q, zero)
    qq_ref[1] = jnp.where(lane >= HEAD_DIM, q, zero)
    acc_ref[...] = jnp.zeros(acc_ref.shape, F32)

    def steps(js, carry, mask_last):
        kcs = [k_ref[0, pl.ds(pl.multiple_of(j * tk, tk), tk), :] for j in js]
        ss = [[lax.dot_general(kc, qq_ref[c], (((1,), (1,)), ((), ())),
                               preferred_element_type=F32) for c in range(2)]
              for kc in kcs]
        carry = list(carry)
        work = []
        for n, j in enumerate(js):
            for c in range(2):
                m_old, l_old = carry[c]
                s = ss[n][c]
                if mask_last and n == len(js) - 1:
                    kpos = j * tk + lax.broadcasted_iota(jnp.int32, s.shape, 0)
                    qpos = i * tq + lax.broadcasted_iota(jnp.int32, s.shape, 1)
                    s = jnp.where(kpos <= qpos, s, NEG_BIG)
                m_new = jnp.maximum(m_old, jnp.max(s, axis=0, keepdims=True))
                alpha = jnp.exp2(m_old - m_new)
                p = jnp.exp2(s - m_new)
                carry[c] = (m_new, alpha * l_old + jnp.sum(p, axis=0, keepdims=True))
                work.append((c, j, alpha, p.astype(BF16)))
        for c, j, alpha, p in work:
            acc_ref[c] = alpha * acc_ref[c] + jnp.dot(vt_ref[j], p, preferred_element_type=F32)
        return tuple(carry)

    stat = (jnp.full((1, tq), NEG_BIG, F32), jnp.zeros((1, tq), F32))
    carry = lax.fori_loop(0, i // 2, lambda jj, c: steps([2 * jj, 2 * jj + 1], c, False),
                          (stat, stat))
    (_, l1), (_, l2) = lax.cond(i % 2 == 1,
                                lambda c: steps([i - 1, i], c, True),
                                lambda c: steps([i], c, True), carry)

    ot = acc_ref[0] * (1.0 / l1) - _lambda(lam_ref, lam_init) * (acc_ref[1] * (1.0 / l2))
    ot = ot * lax.rsqrt(jnp.mean(ot * ot, axis=0, keepdims=True) + EPS)
    ot = ot * (g_ref[...] * (1.0 - lam_init))
    o_ref[0] = ot.T.astype(o_ref.dtype)


def _flash_attn(q, k, vt, lam, g_col, j, lam_init):
    bsz, s, d = q.shape
    tile = vt.shape[4]
    assert s % tile == 0 and vt.shape[2] == s // tile
    q_spec = pl.BlockSpec((1, tile, HEAD_W), lambda bi, h, i: (bi, i, h))
    return pl.pallas_call(
        functools.partial(_flash_kernel, lam_init=lam_init),
        grid=(bsz, d // HEAD_W, s // tile),
        in_specs=[
            pl.BlockSpec((None, 4, HEAD_DIM), lambda bi, h, i: (j, 0, 0)),
            pl.BlockSpec((None, HEAD_W, 1), lambda bi, h, i: (j, 0, 0)),
            q_spec,
            pl.BlockSpec((1, s, HEAD_W), lambda bi, h, i: (bi, 0, h)),
            pl.BlockSpec((None, None, s // tile, HEAD_W, tile), lambda bi, h, i: (bi, h, 0, 0, 0)),
        ],
        out_specs=q_spec,
        out_shape=jax.ShapeDtypeStruct((bsz, s, d), BF16),
        scratch_shapes=[pltpu.VMEM((2, tile, HEAD_W), BF16), pltpu.VMEM((2, HEAD_W, tile), F32)],
        compiler_params=_params(3),
        name="flash_diff_attn",
    )(lam, g_col, q, k, vt)


def _decode_kernel(lam_ref, g_ref, q_ref, kn_ref, vn_ref, k_ref, v_ref, o_ref, *, lam_init):
    n_heads = q_ref.shape[1]
    q = q_ref[0].astype(F32)
    shape = (2 * n_heads, HEAD_W)
    half = lax.broadcasted_iota(jnp.int32, shape, 1) // HEAD_DIM
    qrows = jnp.where(half == lax.broadcasted_iota(jnp.int32, shape, 0) // n_heads,
                      jnp.concatenate([q, q], axis=0), 0.0)
    s = lax.dot_general(qrows.astype(BF16), k_ref[0], (((1,), (1,)), ((), ())),
                        preferred_element_type=F32)
    own = (lax.broadcasted_iota(jnp.int32, s.shape, 1) % n_heads ==
           lax.broadcasted_iota(jnp.int32, s.shape, 0) % n_heads)
    s = jnp.where(own, s, NEG_BIG)
    kn = kn_ref[0]
    vn = vn_ref[0]
    s_new = jnp.sum(qrows * jnp.concatenate([kn, kn], axis=0), axis=1, keepdims=True)
    m = jnp.maximum(jnp.max(s, axis=1, keepdims=True), s_new)
    p = jnp.exp2(s - m)
    p_new = jnp.exp2(s_new - m)
    inv_l = 1.0 / (jnp.sum(p, axis=1, keepdims=True) + p_new)
    o = jnp.dot(p.astype(BF16), v_ref[0], preferred_element_type=F32)
    o = (o + p_new * jnp.concatenate([vn, vn], axis=0)) * inv_l
    res = o[:n_heads] - _lambda(lam_ref, lam_init) * o[n_heads:]
    o_ref[0] = _sub_norm(res, g_ref[...], lam_init).astype(o_ref.dtype)


def _decode_attn(q, k_new, v_new, k_slab, v_slab, first_seq, lam, g, j, lam_init):
    n, rows, _ = k_slab.shape
    n_heads = q.shape[1]
    assert first_seq % n == 0
    row = pl.BlockSpec((1, n_heads, HEAD_W), lambda b: (first_seq + b, 0, 0))
    slab = pl.BlockSpec((1, rows, HEAD_W), lambda b: (b, 0, 0))
    return pl.pallas_call(
        functools.partial(_decode_kernel, lam_init=lam_init),
        grid=(n,),
        in_specs=[
            pl.BlockSpec((None, 4, HEAD_DIM), lambda b: (j, 0, 0)),
            pl.BlockSpec((None, 1, HEAD_W), lambda b: (j, 0, 0)),
            row, row, row, slab, slab,
        ],
        out_specs=pl.BlockSpec((1, n_heads, HEAD_W), lambda b: (b, 0, 0)),
        out_shape=jax.ShapeDtypeStruct((n, n_heads, HEAD_W), BF16),
        compiler_params=_params(1),
        name="decode_attn",
    )(lam, g, q, k_new, v_new, k_slab, v_slab)


def _rope_tables(pos):
    half = HEAD_DIM // 2
    freqs = ROPE_THETA ** (-jnp.arange(half, dtype=F32) * 2.0 / HEAD_DIM)
    ang = pos.astype(F32)[:, None] * freqs[None, :]
    cos, sin = jnp.cos(ang), jnp.sin(ang)
    return jnp.tile(cos, (1, 4)), jnp.tile(jnp.concatenate([-sin, sin], axis=1), (1, 2))


def kernel(x_prompt, x_sample, state_conv, cache_k, cache_v, page_table, norm_ffn, ffn_w_in, ffn_w_out, norm_mix, conv_w_in, conv_b_in, conv_dw, conv_dw_b, conv_ln_g, conv_ln_b, conv_w_out, conv_b_out, kv_norm, w_kv, w_q, lambda_q1, lambda_k1, lambda_q2, lambda_k2, subln_g, w_o, norm_final):
    bsz, s, d = x_prompt.shape
    n_dec, dec_seq, _ = x_sample.shape
    assert dec_seq == 1
    depth = norm_mix.shape[0]
    n_a = conv_w_in.shape[0]
    n_heads = d // HEAD_W
    hist = CONV_WIDTH - 1
    past_len = page_table.shape[1] * cache_k.shape[1]

    w_in = ffn_w_in.astype(BF16)
    w_out = ffn_w_out.astype(BF16)
    cw_in = conv_w_in.astype(BF16)
    cw_out = conv_w_out.astype(BF16)
    wkv = w_kv.astype(BF16)
    wq = w_q.astype(BF16)
    wo = w_o.astype(BF16)
    g_ffn = norm_ffn[:, :, None, :]
    g_mix = norm_mix[:, None, :]
    g_kv = kv_norm[None, :]
    g_fin = norm_final[None, :]
    cb_in = conv_b_in[:, None, :]
    cdw_b = conv_dw_b[:, None, :]
    cln_g = conv_ln_g[:, None, :]
    cln_b = conv_ln_b[:, None, :]
    cb_out = conv_b_out[:, None, :]
    lam = jnp.stack([lambda_q1, lambda_k1, lambda_q2, lambda_k2], axis=1)
    g_sub = subln_g[:, None, :]
    g_sub_col = subln_g[:, :, None]

    q_scale = HEAD_DIM ** -0.5 * math.log2(math.e)

    def run(x, cos, sin, pos_tiles, ffn_fn, conv_fn, attn_fn, attn_layout):
        conv_rows = []
        k = v = kv_ctx = None
        for l in range(depth):
            if l == n_a:
                kv_ctx = _kv_proj(x, g_kv, wkv, cos, sin, pos_tiles, attn_layout)
                k, v = kv_ctx[:2]
            x = ffn_fn(x, l, 0)
            if l < n_a:
                u = _glu_proj(x, g_mix, cw_in, cb_in, l)
                x = conv_fn(u, x, l)
                conv_rows.append(u)
            else:
                j = l - n_a
                lam_init = 0.8 - 0.6 * math.exp(-0.3 * l)
                o = attn_fn(x, kv_ctx, l, j, lam_init)
                x = _out_proj(o, wo, x, j)
            x = ffn_fn(x, l, 1)
        return x, conv_rows, k, v

    def final_gain(l, i):
        return g_fin if (l == depth - 1 and i == 1) else None

    cos_p, sin_p = _rope_tables(jnp.arange(s, dtype=jnp.int32))
    tiles_p = s // min(TOKEN_TILE, s)

    n_groups = depth
    group = n_dec // n_groups
    assert n_dec % n_groups == 0
    pool, page = cache_k.shape[:2]
    caches = (cache_k.reshape(pool, page * n_heads, HEAD_W),
              cache_v.reshape(pool, page * n_heads, HEAD_W))
    slabs = ([], [])

    def ffn_prompt(x, l, i):
        call = 2 * l + i
        which, grp = divmod(call, n_groups)
        y, slab = _ffn(x, g_ffn, w_in, w_out, l, i, final_gain(l, i),
                       pages=(page_table, caches[which], grp * group, group))
        slabs[which].append(slab)
        return y

    def conv_prompt(u, x, l):
        y = _conv_seq(u.reshape(bsz, s, d), x.reshape(bsz, s, d), conv_dw, cdw_b, cln_g, cln_b,
                      cw_out, cb_out, l)
        return y.reshape(bsz * s, d)

    def attn_prompt(x, kv_ctx, l, j, lam_init):
        q = _q_proj(x, g_mix, wq, cos_p, sin_p, tiles_p, l, j, q_scale)
        _, _, kb, vt = kv_ctx
        o = _flash_attn(q.reshape(bsz, s, d), kb.reshape(bsz, s, d), vt, lam, g_sub_col, j,
                        lam_init)
        return o.reshape(bsz * s, d)

    y_p, conv_p, k_p, v_p = run(x_prompt.reshape(bsz * s, d), cos_p, sin_p, tiles_p,
                                ffn_prompt, conv_prompt, attn_prompt, True)
    conv_p = jnp.stack([u.reshape(bsz, s, d)[:, s - hist:, :] for u in conv_p], axis=0)

    cos_s, sin_s = _rope_tables(jnp.full((n_dec,), past_len, dtype=jnp.int32))

    def conv_sample(u, x, l):
        return _conv_step(state_conv, u, x, conv_dw, cdw_b, cln_g, cln_b, cw_out, cb_out, l)

    def ffn_sample(x, l, i):
        return _ffn(x, g_ffn, w_in, w_out, l, i, final_gain(l, i))

    def attn_sample(x, kv_ctx, l, j, lam_init):
        q = _q_proj(x, g_mix, wq, cos_s, sin_s, 1, l, j, q_scale)
        k, v = kv_ctx
        per_head = (n_dec, n_heads, HEAD_W)
        rows = [_decode_attn(q.reshape(per_head), k.reshape(per_head), v.reshape(per_head),
                             slabs[0][grp], slabs[1][grp], grp * group, lam, g_sub, j, lam_init)
                for grp in range(n_groups)]
        return jnp.concatenate(rows, axis=0).reshape(n_dec, d)

    y_s, conv_s, k_s, v_s = run(x_sample.reshape(n_dec, d), cos_s, sin_s, 1,
                                ffn_sample, conv_sample, attn_sample, False)
    conv_s = jnp.stack([jnp.concatenate([state_conv[l][:, 1:, :], u[:, None, :]], axis=1)
                        for l, u in enumerate(conv_s)], axis=0)

    return (y_p.reshape(bsz, s, d), y_s.reshape(n_dec, 1, d), conv_p,
            k_p.reshape(bsz, s, n_heads, HEAD_W), v_p.reshape(bsz, s, n_heads, HEAD_W),
            conv_s, k_s.reshape(n_dec, 1, n_heads, HEAD_W), v_s.reshape(n_dec, 1, n_heads, HEAD_W))
```

```python
import functools
import math

import jax
import jax.numpy as jnp
from jax import lax
from jax.experimental import pallas as pl
from jax.experimental.pallas import tpu as pltpu

HEAD_DIM = 64
HEAD_W = 2 * HEAD_DIM
CONV_WIDTH = 31
CONV_HALO = 32
ROPE_THETA = 10000.0
EPS = 1e-6
NEG_BIG = -1e30

TOKEN_TILE = 512
PROJ_TILE = 1024
FFN_CHUNK = 256
DECODE_SEQS = 2
CONV_ROWS = 8
CONV_TAIL = CONV_ROWS
VMEM_LIMIT = 60 * 1024 * 1024

F32 = jnp.float32
BF16 = jnp.bfloat16


def _params(n_axes, vmem=VMEM_LIMIT):
    return pltpu.CompilerParams(dimension_semantics=("arbitrary",) * n_axes,
                                vmem_limit_bytes=vmem)


def _rms(x, g):
    return x * lax.rsqrt(jnp.mean(x * x, axis=-1, keepdims=True) + EPS) * g


def _resident(shape, index_map):
    return pl.BlockSpec(shape, index_map, pipeline_mode=pl.Buffered(1))


def _compact_pages(page_refs, out_ref, n_pages):
    for idx, ref in enumerate(page_refs):
        sidx, p = divmod(idx, n_pages)
        rows = ref.shape[1]
        out_ref[sidx, p * rows:(p + 1) * rows, :] = ref[0].astype(BF16)


def _ffn_kernel(*refs, final_norm, attn_in, q_scale, n_page_refs, n_pages):
    refs = list(refs)
    if n_page_refs:
        refs.pop(0)
    x_ref, g_ref, wg_ref, wu_ref, wo_ref = refs[:5]
    del refs[:5]
    if final_norm:
        gf_ref = refs.pop(0)
    if attn_in:
        a_ref, wa_ref = refs[:2]
        del refs[:2]
    if q_scale is not None:
        gq_ref, wq_ref, cos_ref, sin_ref = refs[:4]
        del refs[:4]
    page_refs = refs[:n_page_refs]
    del refs[:n_page_refs]
    o_ref = refs.pop(0)
    if q_scale is not None:
        q_ref = refs.pop(0)
    if n_page_refs:
        _compact_pages(page_refs, refs.pop(0), n_pages)
    xn_ref, h_ref = refs
    x = x_ref[...]
    if attn_in:
        x = x + jnp.dot(a_ref[...], wa_ref[...], preferred_element_type=F32)
    xn_ref[...] = _rms(x, g_ref[...]).astype(BF16)
    d_ff = h_ref.shape[1]
    for c in range(0, d_ff, FFN_CHUNK):
        xn = xn_ref[...]
        gate = jnp.dot(xn, wg_ref[:, c:c + FFN_CHUNK], preferred_element_type=F32)
        up = jnp.dot(xn, wu_ref[:, c:c + FFN_CHUNK], preferred_element_type=F32)
        h_ref[:, c:c + FFN_CHUNK] = (gate * jax.nn.sigmoid(gate) * up).astype(BF16)
    y = x + 0.5 * jnp.dot(h_ref[...], wo_ref[...], preferred_element_type=F32)
    if final_norm:
        y = _rms(y, gf_ref[...])
    o_ref[...] = y
    if q_scale is not None:
        yq = jnp.dot(_rms(y, gq_ref[...]).astype(BF16), wq_ref[...], preferred_element_type=F32)
        cos = cos_ref[...]
        sin = sin_ref[...]
        for h in range(yq.shape[1] // HEAD_W):
            sl = slice(h * HEAD_W, (h + 1) * HEAD_W)
            q_ref[:, sl] = (_rope_head(yq[:, sl], cos, sin) * q_scale).astype(q_ref.dtype)


def _ffn(x, gains, w_in, w_out, l, i, final_gain=None, pages=None, attn=None, query=None):
    t, d = x.shape
    f = w_out.shape[2]
    assert f % FFN_CHUNK == 0
    tm = min(TOKEN_TILE, t)
    assert t % tm == 0
    steps = t // tm
    tok = pl.BlockSpec((tm, d), lambda r, *_: (r, 0))
    in_specs = [
        tok,
        _resident((None, None, 1, d), lambda r, *_: (l, i, 0, 0)),
        _resident((None, None, d, f), lambda r, *_: (l, i, 0, 0)),
        _resident((None, None, d, f), lambda r, *_: (l, i, 0, 1)),
        _resident((None, None, f, d), lambda r, *_: (l, i, 0, 0)),
    ]
    args = [x, gains, w_in, w_in, w_out]
    out_specs = [tok]
    out_shape = [jax.ShapeDtypeStruct((t, d), F32)]
    if final_gain is not None:
        in_specs.append(_resident((1, d), lambda r, *_: (0, 0)))
        args.append(final_gain)
    if attn is not None:
        a, wa, ja = attn
        in_specs += [tok, _resident((None, d, d), lambda r, *_: (ja, 0, 0))]
        args += [a, wa]
    q_scale = None
    if query is not None:
        gq, wq, jq, cos, sin, q_scale = query
        pos_tiles = cos.shape[0] // tm
        pos = pl.BlockSpec((tm, HEAD_W), lambda r, *_: (r % pos_tiles, 0))
        in_specs += [_resident((None, 1, d), lambda r, *_: (l, 0, 0)),
                     _resident((None, d, d), lambda r, *_: (jq, 0, 0)), pos, pos]
        args += [gq, wq, cos, sin]
        out_specs.append(tok)
        out_shape.append(jax.ShapeDtypeStruct((t, d), BF16))
    n_page_refs = n_pages = 0
    if pages is not None:
        page_table, cache, first_seq, n_seq = pages
        n_pages = page_table.shape[1]
        rows = cache.shape[1]
        assert n_seq % steps == 0
        per_step = n_seq // steps
        for sq in range(per_step):
            for p in range(n_pages):
                in_specs.append(pl.BlockSpec(
                    (1, rows, HEAD_W),
                    lambda r, pt, sq=sq, p=p: (pt[first_seq + r * per_step + sq, p], 0, 0)))
                args.append(cache)
        n_page_refs = per_step * n_pages
        out_specs.append(pl.BlockSpec((per_step, n_pages * rows, HEAD_W),
                                      lambda r, pt: (r, 0, 0)))
        out_shape.append(jax.ShapeDtypeStruct((n_seq, n_pages * rows, HEAD_W), BF16))
        args = [page_table] + args
    grid_spec = pltpu.PrefetchScalarGridSpec(
        num_scalar_prefetch=0 if pages is None else 1,
        grid=(steps,),
        in_specs=in_specs,
        out_specs=out_specs,
        scratch_shapes=[pltpu.VMEM((tm, d), BF16), pltpu.VMEM((tm, f), BF16)],
    )
    return pl.pallas_call(
        functools.partial(_ffn_kernel, final_norm=final_gain is not None,
                          attn_in=attn is not None, q_scale=q_scale,
                          n_page_refs=n_page_refs, n_pages=n_pages),
        grid_spec=grid_spec,
        out_shape=out_shape,
        compiler_params=_params(1),
        name="ffn",
    )(*args)


def _glu_kernel(x_ref, g_ref, w_ref, b_ref, u_ref):
    d = u_ref.shape[1]
    xn = _rms(x_ref[...], g_ref[...]).astype(BF16)
    y = jnp.dot(xn, w_ref[...], preferred_element_type=F32) + b_ref[...]
    u_ref[...] = y[:, :d] * jax.nn.sigmoid(y[:, d:])


def _glu_proj(x, gains, w, b, l):
    t, d = x.shape
    tm = min(PROJ_TILE, t)
    return pl.pallas_call(
        _glu_kernel,
        grid=(t // tm,),
        in_specs=[
            pl.BlockSpec((tm, d), lambda r: (r, 0)),
            _resident((None, 1, d), lambda r: (l, 0, 0)),
            _resident((None, d, 2 * d), lambda r: (l, 0, 0)),
            _resident((None, 1, 2 * d), lambda r: (l, 0, 0)),
        ],
        out_specs=pl.BlockSpec((tm, d), lambda r: (r, 0)),
        out_shape=jax.ShapeDtypeStruct((t, d), F32),
        compiler_params=_params(1),
        name="conv_glu",
    )(x, gains, w, b)


def _rope_head(xh, cos, sin_signed):
    half = HEAD_DIM // 2
    lane = lax.broadcasted_iota(jnp.int32, xh.shape, 1)
    partner = jnp.where(lane % HEAD_DIM < half,
                        pltpu.roll(xh, HEAD_W - half, 1),
                        pltpu.roll(xh, half, 1))
    return xh * cos + partner * sin_signed


def _kv_kernel(x_ref, g_ref, w_ref, cos_ref, sin_ref, k_ref, v_ref, *attn_refs):
    d = k_ref.shape[1]
    xn = _rms(x_ref[...], g_ref[...]).astype(BF16)
    y = jnp.dot(xn, w_ref[...], preferred_element_type=F32)
    v_ref[...] = y[:, d:]
    cos = cos_ref[...]
    sin = sin_ref[...]
    for h in range(d // HEAD_W):
        sl = slice(h * HEAD_W, (h + 1) * HEAD_W)
        kh = _rope_head(y[:, sl], cos, sin)
        k_ref[:, sl] = kh
        if attn_refs:
            kb_ref, vt_ref = attn_refs
            kb_ref[:, sl] = kh.astype(BF16)
            vt_ref[h] = y[:, d + h * HEAD_W:d + (h + 1) * HEAD_W].T.astype(BF16)


def _kv_proj(x, gain, w, cos, sin, pos_tiles, attn_layout):
    t, d = x.shape
    tm = min(TOKEN_TILE, t)
    tok = pl.BlockSpec((tm, d), lambda r: (r, 0))
    pos = pl.BlockSpec((tm, HEAD_W), lambda r: (r % pos_tiles, 0))
    out_specs = [tok, tok]
    out_shape = [jax.ShapeDtypeStruct((t, d), F32), jax.ShapeDtypeStruct((t, d), F32)]
    if attn_layout:
        n_heads = d // HEAD_W
        out_specs += [tok, pl.BlockSpec((None, n_heads, None, HEAD_W, tm),
                                        lambda r: (r // pos_tiles, 0, r % pos_tiles, 0, 0))]
        out_shape += [jax.ShapeDtypeStruct((t, d), BF16),
                      jax.ShapeDtypeStruct((t // (pos_tiles * tm), n_heads, pos_tiles, HEAD_W, tm),
                                           BF16)]
    return pl.pallas_call(
        _kv_kernel,
        grid=(t // tm,),
        in_specs=[tok, _resident((1, d), lambda r: (0, 0)),
                  _resident((d, 2 * d), lambda r: (0, 0)), pos, pos],
        out_specs=out_specs,
        out_shape=out_shape,
        compiler_params=_params(1),
        name="kv_proj",
    )(x, gain, w, cos, sin)


def _conv_tail(c, x, lng_ref, lnb_ref, w_ref, b_ref):
    mu = jnp.mean(c, axis=-1, keepdims=True)
    xc = c - mu
    y = xc * lax.rsqrt(jnp.mean(xc * xc, axis=-1, keepdims=True) + EPS)
    y = y * lng_ref[...] + lnb_ref[...]
    z = (y * jax.nn.sigmoid(y)).astype(BF16)
    return x + jnp.dot(z, w_ref[...], preferred_element_type=F32) + b_ref[...]


def _conv_seq_kernel(u_ref, halo_ref, x_ref, dw_ref, dwb_ref, lng_ref, lnb_ref, w_ref, b_ref,
                     o_ref, st_ref, full_ref, c_ref):
    ts, d = c_ref.shape
    first = pl.program_id(1) == 0

    @pl.when(pl.program_id(1) == pl.num_programs(1) - 1)
    def _():
        hist = st_ref.shape[1]
        st_ref[0] = u_ref[0, ts - hist:ts, :]

    @pl.when(first)
    def _():
        full_ref[0:CONV_HALO, :] = jnp.zeros((CONV_HALO, d), F32)

    @pl.when(jnp.logical_not(first))
    def _():
        full_ref[0:CONV_HALO, :] = halo_ref[0]

    full_ref[CONV_HALO:CONV_HALO + ts, :] = u_ref[0]
    full_ref[CONV_HALO + ts:, :] = jnp.zeros((CONV_TAIL, d), F32)
    lead = CONV_HALO - (CONV_WIDTH - 1)

    n_a = (lead + CONV_WIDTH - 1) // CONV_ROWS + 1
    sub = lax.broadcasted_iota(jnp.int32, (CONV_ROWS, HEAD_W), 0)

    for cb in range(d // HEAD_W):
        sl = slice(cb * HEAD_W, (cb + 1) * HEAD_W)
        taps = {}
        for a in range(n_a):
            for b in range(CONV_ROWS):
                j = CONV_ROWS * a + b - lead
                if 0 <= j < CONV_WIDTH:
                    taps[a, b] = jnp.broadcast_to(dw_ref[j:j + 1, sl], (CONV_ROWS, HEAD_W))
        bias = jnp.broadcast_to(dwb_ref[:, sl], (CONV_ROWS, HEAD_W))

        def partials(r):
            base = pl.multiple_of(r * CONV_ROWS, CONV_ROWS)
            rows = [full_ref[pl.ds(base + CONV_ROWS * a, CONV_ROWS), sl] for a in range(n_a)]
            out = []
            for b in range(CONV_ROWS):
                z = None
                for a in range(n_a):
                    if (a, b) in taps:
                        term = rows[a] * taps[a, b]
                        z = term if z is None else z + term
                out.append(z)
            return tuple(out)

        def block(r, z_prev):
            z_next = partials(r)
            acc = bias + z_prev[0]
            for b in range(1, CONV_ROWS):
                mixed = jnp.where(sub >= b, z_prev[b], z_next[b])
                acc = acc + pltpu.roll(mixed, CONV_ROWS - b, 0)
            c_ref[pl.ds(pl.multiple_of((r - 1) * CONV_ROWS, CONV_ROWS), CONV_ROWS), sl] = acc
            return z_next

        lax.fori_loop(1, ts // CONV_ROWS + 1, block, partials(0), unroll=2)

    o_ref[0] = _conv_tail(c_ref[...], x_ref[0], lng_ref, lnb_ref, w_ref, b_ref)


def _conv_seq(u, x, dw, dwb, lng, lnb, w, b, l):
    bsz, s, d = u.shape
    hist = CONV_WIDTH - 1
    ts = min(TOKEN_TILE, s)
    assert s % ts == 0 and ts % CONV_HALO == 0
    hb = ts // CONV_HALO
    tile = pl.BlockSpec((1, ts, d), lambda bi, i: (bi, i, 0))
    vec = _resident((None, 1, d), lambda bi, i: (l, 0, 0))
    return pl.pallas_call(
        _conv_seq_kernel,
        grid=(bsz, s // ts),
        in_specs=[
            tile,
            pl.BlockSpec((1, CONV_HALO, d), lambda bi, i: (bi, jnp.maximum(i * hb - 1, 0), 0)),
            tile,
            _resident((None, CONV_WIDTH, d), lambda bi, i: (l, 0, 0)),
            vec, vec, vec,
            _resident((None, d, d), lambda bi, i: (l, 0, 0)),
            vec,
        ],
        out_specs=[tile, pl.BlockSpec((1, hist, d), lambda bi, i: (bi, 0, 0))],
        out_shape=[jax.ShapeDtypeStruct((bsz, s, d), F32),
                   jax.ShapeDtypeStruct((bsz, hist, d), F32)],
        scratch_shapes=[pltpu.VMEM((ts + CONV_HALO + CONV_TAIL, d), F32), pltpu.VMEM((ts, d), F32)],
        compiler_params=_params(2),
        name="conv_seq",
    )(u, u, x, dw, dwb, lng, lnb, w, b)


def _conv_step_kernel(st_ref, u_ref, x_ref, dw_ref, dwb_ref, lng_ref, lnb_ref, w_ref, b_ref,
                      o_ref, ns_ref, c_ref):
    nb = c_ref.shape[0]
    hist = CONV_WIDTH - 1
    w_hist = dw_ref[0:hist, :]
    for r in range(nb):
        c_ref[r:r + 1, :] = jnp.sum(st_ref[r] * w_hist, axis=0, keepdims=True)
        ns_ref[r, 0:hist - 1, :] = st_ref[r, 1:hist, :]
        ns_ref[r, hist - 1:hist, :] = u_ref[r:r + 1, :]
    c = c_ref[...] + u_ref[...] * dw_ref[hist:hist + 1, :] + dwb_ref[...]
    o_ref[...] = _conv_tail(c, x_ref[...], lng_ref, lnb_ref, w_ref, b_ref)


def _conv_step(state, u, x, dw, dwb, lng, lnb, w, b, l):
    n, d = u.shape
    nb = min(16, n)
    assert n % nb == 0
    hist = CONV_WIDTH - 1
    tok = pl.BlockSpec((nb, d), lambda r: (r, 0))
    vec = _resident((None, 1, d), lambda r: (l, 0, 0))
    return pl.pallas_call(
        _conv_step_kernel,
        grid=(n // nb,),
        in_specs=[
            pl.BlockSpec((None, nb, hist, d), lambda r: (l, r, 0, 0)),
            tok, tok,
            _resident((None, CONV_WIDTH, d), lambda r: (l, 0, 0)),
            vec, vec, vec,
            _resident((None, d, d), lambda r: (l, 0, 0)),
            vec,
        ],
        out_specs=[tok, pl.BlockSpec((nb, hist, d), lambda r: (r, 0, 0))],
        out_shape=[jax.ShapeDtypeStruct((n, d), F32), jax.ShapeDtypeStruct((n, hist, d), F32)],
        scratch_shapes=[pltpu.VMEM((nb, d), F32)],
        compiler_params=_params(1),
        name="conv_step",
    )(state, u, x, dw, dwb, lng, lnb, w, b)


def _lambda(lam_ref, lam_init):
    lv = lam_ref[...]
    a = jnp.sum(lv[0:1] * lv[1:2], axis=-1, keepdims=True)
    b = jnp.sum(lv[2:3] * lv[3:4], axis=-1, keepdims=True)
    return jnp.exp(a) - jnp.exp(b) + lam_init


def _sub_norm(o, g, lam_init):
    return o * lax.rsqrt(jnp.mean(o * o, axis=-1, keepdims=True) + EPS) * g * (1.0 - lam_init)


def _flash_kernel(lam_ref, g_ref, q_ref, k_ref, vt_ref, o_ref, qq_ref, acc_ref, *, lam_init):
    tq = q_ref.shape[1]
    tk = vt_ref.shape[2]
    i = pl.program_id(2)
    q = q_ref[0]
    lane = lax.broadcasted_iota(jnp.int32, q.shape, 1)
    zero = jnp.zeros_like(q)
    qq_ref[0] = jnp.where(lane < HEAD_DIM, q, zero)
    qq_ref[1] = jnp.where(lane >= HEAD_DIM, q, zero)
    acc_ref[...] = jnp.zeros(acc_ref.shape, F32)

    def steps(js, carry, mask_last):
        half = tq // 2
        pieces = []
        for n, j in enumerate(js):
            if mask_last and n == len(js) - 1:
                pieces += [(j, slice(0, half), slice(0, half), True),
                           (j, slice(half, tq), slice(0, tk), True)]
            else:
                pieces.append((j, slice(0, tq), slice(0, tk), False))
        ss = []
        for j, cols, keys, _ in pieces:
            kc = k_ref[0, pl.ds(pl.multiple_of(j * tk, tk) + keys.start, keys.stop - keys.start), :]
            ss.append([lax.dot_general(kc, qq_ref[c, cols, :], (((1,), (1,)), ((), ())),
                                       preferred_element_type=F32) for c in range(2)])
        carry = [list(st) for st in carry]
        work = []
        for (j, cols, keys, masked), s_pair in zip(pieces, ss):
            for c in range(2):
                m_old, l_old = carry[c][0][:, cols], carry[c][1][:, cols]
                s = s_pair[c]
                if masked:
                    kpos = keys.start + lax.broadcasted_iota(jnp.int32, s.shape, 0)
                    qpos = cols.start + lax.broadcasted_iota(jnp.int32, s.shape, 1)
                    s = jnp.where(kpos <= qpos, s, NEG_BIG)
                m_new = jnp.maximum(m_old, jnp.max(s, axis=0, keepdims=True))
                alpha = jnp.exp2(m_old - m_new)
                p = jnp.exp2(s - m_new)
                l_new = alpha * l_old + jnp.sum(p, axis=0, keepdims=True)
                for idx, new in ((0, m_new), (1, l_new)):
                    old = carry[c][idx]
                    parts = ([old[:, :cols.start]] if cols.start else []) + [new]
                    parts += [old[:, cols.stop:]] if cols.stop < tq else []
                    carry[c][idx] = jnp.concatenate(parts, axis=1)
                work.append((c, j, cols, keys, alpha, p.astype(BF16)))
        for c, j, cols, keys, alpha, p in work:
            acc_ref[c, :, cols] = alpha * acc_ref[c, :, cols] + jnp.dot(
                vt_ref[j, :, keys], p, preferred_element_type=F32)
        return tuple(tuple(st) for st in carry)

    stat = (jnp.full((1, tq), NEG_BIG, F32), jnp.zeros((1, tq), F32))
    carry = lax.fori_loop(0, i // 2, lambda jj, c: steps([2 * jj, 2 * jj + 1], c, False),
                          (stat, stat))
    (_, l1), (_, l2) = lax.cond(i % 2 == 1,
                                lambda c: steps([i - 1, i], c, True),
                                lambda c: steps([i], c, True), carry)

    ot = acc_ref[0] * (1.0 / l1) - _lambda(lam_ref, lam_init) * (acc_ref[1] * (1.0 / l2))
    ot = ot * lax.rsqrt(jnp.mean(ot * ot, axis=0, keepdims=True) + EPS)
    ot = ot * (g_ref[...] * (1.0 - lam_init))
    o_ref[0] = ot.T.astype(o_ref.dtype)


def _flash_attn(q, k, vt, lam, g_col, j, lam_init):
    bsz, s, d = q.shape
    tile = vt.shape[4]
    assert s % tile == 0 and vt.shape[2] == s // tile
    q_spec = pl.BlockSpec((1, tile, HEAD_W), lambda bi, h, i: (bi, i, h))
    return pl.pallas_call(
        functools.partial(_flash_kernel, lam_init=lam_init),
        grid=(bsz, d // HEAD_W, s // tile),
        in_specs=[
            pl.BlockSpec((None, 4, HEAD_DIM), lambda bi, h, i: (j, 0, 0)),
            pl.BlockSpec((None, HEAD_W, 1), lambda bi, h, i: (j, 0, 0)),
            q_spec,
            pl.BlockSpec((1, s, HEAD_W), lambda bi, h, i: (bi, 0, h)),
            pl.BlockSpec((None, None, s // tile, HEAD_W, tile), lambda bi, h, i: (bi, h, 0, 0, 0)),
        ],
        out_specs=q_spec,
        out_shape=jax.ShapeDtypeStruct((bsz, s, d), BF16),
        scratch_shapes=[pltpu.VMEM((2, tile, HEAD_W), BF16), pltpu.VMEM((2, HEAD_W, tile), F32)],
        compiler_params=_params(3),
        name="flash_diff_attn",
    )(lam, g_col, q, k, vt)


def _decode_kernel(lam_ref, g_ref, q_ref, kn_ref, vn_ref, k_ref, v_ref, o_ref, *, lam_init):
    n_heads = q_ref.shape[1]
    shape = (2 * n_heads, HEAD_W)
    own_half = (lax.broadcasted_iota(jnp.int32, shape, 1) // HEAD_DIM ==
                lax.broadcasted_iota(jnp.int32, shape, 0) // n_heads)
    lam = _lambda(lam_ref, lam_init)
    for sq in range(q_ref.shape[0]):
        q = q_ref[sq].astype(F32)
        qrows = jnp.where(own_half, jnp.concatenate([q, q], axis=0), 0.0)
        s = lax.dot_general(qrows.astype(BF16), k_ref[sq], (((1,), (1,)), ((), ())),
                            preferred_element_type=F32)
        own = (lax.broadcasted_iota(jnp.int32, s.shape, 1) % n_heads ==
               lax.broadcasted_iota(jnp.int32, s.shape, 0) % n_heads)
        s = jnp.where(own, s, NEG_BIG)
        kn = kn_ref[sq]
        vn = vn_ref[sq]
        s_new = jnp.sum(qrows * jnp.concatenate([kn, kn], axis=0), axis=1, keepdims=True)
        m = jnp.maximum(jnp.max(s, axis=1, keepdims=True), s_new)
        p = jnp.exp2(s - m)
        p_new = jnp.exp2(s_new - m)
        inv_l = 1.0 / (jnp.sum(p, axis=1, keepdims=True) + p_new)
        o = jnp.dot(p.astype(BF16), v_ref[sq], preferred_element_type=F32)
        o = (o + p_new * jnp.concatenate([vn, vn], axis=0)) * inv_l
        res = o[:n_heads] - lam * o[n_heads:]
        o_ref[sq] = _sub_norm(res, g_ref[...], lam_init).astype(o_ref.dtype)


def _decode_attn(q, k_new, v_new, k_slab, v_slab, first_seq, lam, g, j, lam_init):
    n, rows, _ = k_slab.shape
    n_heads = q.shape[1]
    per = DECODE_SEQS if n % DECODE_SEQS == 0 else 1
    assert first_seq % per == 0
    row = pl.BlockSpec((per, n_heads, HEAD_W), lambda b: (first_seq // per + b, 0, 0))
    slab = pl.BlockSpec((per, rows, HEAD_W), lambda b: (b, 0, 0))
    return pl.pallas_call(
        functools.partial(_decode_kernel, lam_init=lam_init),
        grid=(n // per,),
        in_specs=[
            pl.BlockSpec((None, 4, HEAD_DIM), lambda b: (j, 0, 0)),
            pl.BlockSpec((None, 1, HEAD_W), lambda b: (j, 0, 0)),
            row, row, row, slab, slab,
        ],
        out_specs=pl.BlockSpec((per, n_heads, HEAD_W), lambda b: (b, 0, 0)),
        out_shape=jax.ShapeDtypeStruct((n, n_heads, HEAD_W), BF16),
        compiler_params=_params(1),
        name="decode_attn",
    )(lam, g, q, k_new, v_new, k_slab, v_slab)


def _rope_tables(pos):
    half = HEAD_DIM // 2
    freqs = ROPE_THETA ** (-jnp.arange(half, dtype=F32) * 2.0 / HEAD_DIM)
    ang = pos.astype(F32)[:, None] * freqs[None, :]
    cos, sin = jnp.cos(ang), jnp.sin(ang)
    return jnp.tile(cos, (1, 4)), jnp.tile(jnp.concatenate([-sin, sin], axis=1), (1, 2))


def kernel(x_prompt, x_sample, state_conv, cache_k, cache_v, page_table, norm_ffn, ffn_w_in, ffn_w_out, norm_mix, conv_w_in, conv_b_in, conv_dw, conv_dw_b, conv_ln_g, conv_ln_b, conv_w_out, conv_b_out, kv_norm, w_kv, w_q, lambda_q1, lambda_k1, lambda_q2, lambda_k2, subln_g, w_o, norm_final):
    bsz, s, d = x_prompt.shape
    n_dec, dec_seq, _ = x_sample.shape
    assert dec_seq == 1
    depth = norm_mix.shape[0]
    n_a = conv_w_in.shape[0]
    n_heads = d // HEAD_W
    past_len = page_table.shape[1] * cache_k.shape[1]

    w_in = ffn_w_in.astype(BF16)
    w_out = ffn_w_out.astype(BF16)
    cw_in = conv_w_in.astype(BF16)
    cw_out = conv_w_out.astype(BF16)
    wkv = w_kv.astype(BF16)
    wq = w_q.astype(BF16)
    wo = w_o.astype(BF16)
    g_ffn = norm_ffn[:, :, None, :]
    g_mix = norm_mix[:, None, :]
    g_kv = kv_norm[None, :]
    g_fin = norm_final[None, :]
    cb_in = conv_b_in[:, None, :]
    cdw_b = conv_dw_b[:, None, :]
    cln_g = conv_ln_g[:, None, :]
    cln_b = conv_ln_b[:, None, :]
    cb_out = conv_b_out[:, None, :]
    lam = jnp.stack([lambda_q1, lambda_k1, lambda_q2, lambda_k2], axis=1)
    g_sub = subln_g[:, None, :]
    g_sub_col = subln_g[:, :, None]

    q_scale = HEAD_DIM ** -0.5 * math.log2(math.e)

    def run(x, cos, sin, pos_tiles, ffn_fn, conv_fn, attn_fn, attn_layout):
        conv_rows = []
        k = v = kv_ctx = None
        for l in range(depth):
            if l == n_a:
                kv_ctx = _kv_proj(x, g_kv, wkv, cos, sin, pos_tiles, attn_layout)
                k, v = kv_ctx[:2]
            if l < n_a:
                x = ffn_fn(x, l, 0)[0]
                u = _glu_proj(x, g_mix, cw_in, cb_in, l)
                x, state = conv_fn(u, x, l)
                conv_rows.append(state)
                x = ffn_fn(x, l, 1)[0]
            else:
                j = l - n_a
                lam_init = 0.8 - 0.6 * math.exp(-0.3 * l)
                x, q = ffn_fn(x, l, 0, query=(g_mix, wq, j, cos, sin, q_scale))[:2]
                o = attn_fn(q, kv_ctx, j, lam_init)
                x = ffn_fn(x, l, 1, attn=(o, wo, j))[0]
        return x, conv_rows, k, v

    def final_gain(l, i):
        return g_fin if (l == depth - 1 and i == 1) else None

    cos_p, sin_p = _rope_tables(jnp.arange(s, dtype=jnp.int32))
    tiles_p = s // min(TOKEN_TILE, s)

    n_groups = depth
    group = n_dec // n_groups
    assert n_dec % n_groups == 0
    pool, page = cache_k.shape[:2]
    caches = (cache_k.reshape(pool, page * n_heads, HEAD_W),
              cache_v.reshape(pool, page * n_heads, HEAD_W))
    slabs = ([], [])

    def ffn_prompt(x, l, i, **fused):
        call = 2 * l + i
        which, grp = divmod(call, n_groups)
        *outs, slab = _ffn(x, g_ffn, w_in, w_out, l, i, final_gain(l, i),
                           pages=(page_table, caches[which], grp * group, group), **fused)
        slabs[which].append(slab)
        return outs

    def conv_prompt(u, x, l):
        y, state = _conv_seq(u.reshape(bsz, s, d), x.reshape(bsz, s, d), conv_dw, cdw_b, cln_g,
                             cln_b, cw_out, cb_out, l)
        return y.reshape(bsz * s, d), state

    def attn_prompt(q, kv_ctx, j, lam_init):
        _, _, kb, vt = kv_ctx
        o = _flash_attn(q.reshape(bsz, s, d), kb.reshape(bsz, s, d), vt, lam, g_sub_col, j,
                        lam_init)
        return o.reshape(bsz * s, d)

    y_p, conv_p, k_p, v_p = run(x_prompt.reshape(bsz * s, d), cos_p, sin_p, tiles_p,
                                ffn_prompt, conv_prompt, attn_prompt, True)
    conv_p = jnp.stack(conv_p, axis=0)

    cos_s, sin_s = _rope_tables(jnp.full((n_dec,), past_len, dtype=jnp.int32))

    def conv_sample(u, x, l):
        return _conv_step(state_conv, u, x, conv_dw, cdw_b, cln_g, cln_b, cw_out, cb_out, l)

    def ffn_sample(x, l, i, **fused):
        return _ffn(x, g_ffn, w_in, w_out, l, i, final_gain(l, i), **fused)

    def attn_sample(q, kv_ctx, j, lam_init):
        k, v = kv_ctx
        per_head = (n_dec, n_heads, HEAD_W)
        rows = [_decode_attn(q.reshape(per_head), k.reshape(per_head), v.reshape(per_head),
                             slabs[0][grp], slabs[1][grp], grp * group, lam, g_sub, j, lam_init)
                for grp in range(n_groups)]
        return jnp.concatenate(rows, axis=0).reshape(n_dec, d)

    y_s, conv_s, k_s, v_s = run(x_sample.reshape(n_dec, d), cos_s, sin_s, 1,
                                ffn_sample, conv_sample, attn_sample, False)
    conv_s = jnp.stack(conv_s, axis=0)

    return (y_p.reshape(bsz, s, d), y_s.reshape(n_dec, 1, d), conv_p,
            k_p.reshape(bsz, s, n_heads, HEAD_W), v_p.reshape(bsz, s, n_heads, HEAD_W),
            conv_s, k_s.reshape(n_dec, 1, n_heads, HEAD_W), v_s.reshape(n_dec, 1, n_heads, HEAD_W))
```

```python
import functools
import math

import jax
import jax.numpy as jnp
from jax import lax
from jax.experimental import pallas as pl
from jax.experimental.pallas import tpu as pltpu

HEAD_DIM = 64
HEAD_W = 2 * HEAD_DIM
CONV_WIDTH = 31
CONV_HALO = 32
ROPE_THETA = 10000.0
EPS = 1e-6
NEG_BIG = -1e30

TOKEN_TILE = 512
PROJ_TILE = 1024
FFN_CHUNK = 256
DECODE_SEQS = 2
CONV_ROWS = 8
CONV_TAIL = CONV_ROWS
VMEM_LIMIT = 60 * 1024 * 1024

F32 = jnp.float32
BF16 = jnp.bfloat16


def _params(n_axes, vmem=VMEM_LIMIT):
    return pltpu.CompilerParams(dimension_semantics=("arbitrary",) * n_axes,
                                vmem_limit_bytes=vmem)


def _rms(x, g):
    return x * lax.rsqrt(jnp.mean(x * x, axis=-1, keepdims=True) + EPS) * g


def _resident(shape, index_map):
    return pl.BlockSpec(shape, index_map, pipeline_mode=pl.Buffered(1))


def _compact_pages(page_refs, out_ref, n_pages):
    for idx, ref in enumerate(page_refs):
        sidx, p = divmod(idx, n_pages)
        rows = ref.shape[1]
        out_ref[sidx, p * rows:(p + 1) * rows, :] = ref[0].astype(BF16)


def _ffn_kernel(*refs, final_norm, attn_in, q_scale, n_page_refs, n_pages):
    refs = list(refs)
    if n_page_refs:
        refs.pop(0)
    x_ref, g_ref, wg_ref, wu_ref, wo_ref = refs[:5]
    del refs[:5]
    if final_norm:
        gf_ref = refs.pop(0)
    if attn_in:
        a_ref, wa_ref = refs[:2]
        del refs[:2]
    if q_scale is not None:
        gq_ref, wq_ref, cos_ref, sin_ref = refs[:4]
        del refs[:4]
    page_refs = refs[:n_page_refs]
    del refs[:n_page_refs]
    o_ref = refs.pop(0)
    if q_scale is not None:
        q_ref = refs.pop(0)
    if n_page_refs:
        _compact_pages(page_refs, refs.pop(0), n_pages)
    xn_ref, h_ref = refs
    x = x_ref[...]
    if attn_in:
        x = x + jnp.dot(a_ref[...], wa_ref[...], preferred_element_type=F32)
    xn_ref[...] = _rms(x, g_ref[...]).astype(BF16)
    d_ff = h_ref.shape[1]
    for c in range(0, d_ff, FFN_CHUNK):
        xn = xn_ref[...]
        gate = jnp.dot(xn, wg_ref[:, c:c + FFN_CHUNK], preferred_element_type=F32)
        up = jnp.dot(xn, wu_ref[:, c:c + FFN_CHUNK], preferred_element_type=F32)
        h_ref[:, c:c + FFN_CHUNK] = (gate * jax.nn.sigmoid(gate) * up).astype(BF16)
    y = x + 0.5 * jnp.dot(h_ref[...], wo_ref[...], preferred_element_type=F32)
    if final_norm:
        y = _rms(y, gf_ref[...])
    o_ref[...] = y
    if q_scale is not None:
        yq = jnp.dot(_rms(y, gq_ref[...]).astype(BF16), wq_ref[...], preferred_element_type=F32)
        cos = cos_ref[...]
        sin = sin_ref[...]
        for h in range(yq.shape[1] // HEAD_W):
            sl = slice(h * HEAD_W, (h + 1) * HEAD_W)
            q_ref[:, sl] = (_rope_head(yq[:, sl], cos, sin) * q_scale).astype(q_ref.dtype)


def _ffn(x, gains, w_in, w_out, l, i, final_gain=None, pages=None, attn=None, query=None):
    t, d = x.shape
    f = w_out.shape[2]
    assert f % FFN_CHUNK == 0
    tm = min(TOKEN_TILE, t)
    assert t % tm == 0
    steps = t // tm
    tok = pl.BlockSpec((tm, d), lambda r, *_: (r, 0))
    in_specs = [
        tok,
        _resident((None, None, 1, d), lambda r, *_: (l, i, 0, 0)),
        _resident((None, None, d, f), lambda r, *_: (l, i, 0, 0)),
        _resident((None, None, d, f), lambda r, *_: (l, i, 0, 1)),
        _resident((None, None, f, d), lambda r, *_: (l, i, 0, 0)),
    ]
    args = [x, gains, w_in, w_in, w_out]
    out_specs = [tok]
    out_shape = [jax.ShapeDtypeStruct((t, d), F32)]
    if final_gain is not None:
        in_specs.append(_resident((1, d), lambda r, *_: (0, 0)))
        args.append(final_gain)
    if attn is not None:
        a, wa, ja = attn
        in_specs += [tok, _resident((None, d, d), lambda r, *_: (ja, 0, 0))]
        args += [a, wa]
    q_scale = None
    if query is not None:
        gq, wq, jq, cos, sin, q_scale = query
        pos_tiles = cos.shape[0] // tm
        pos = pl.BlockSpec((tm, HEAD_W), lambda r, *_: (r % pos_tiles, 0))
        in_specs += [_resident((None, 1, d), lambda r, *_: (l, 0, 0)),
                     _resident((None, d, d), lambda r, *_: (jq, 0, 0)), pos, pos]
        args += [gq, wq, cos, sin]
        out_specs.append(tok)
        out_shape.append(jax.ShapeDtypeStruct((t, d), BF16))
    n_page_refs = n_pages = 0
    if pages is not None:
        page_table, cache, first_seq, n_seq = pages
        n_pages = page_table.shape[1]
        rows = cache.shape[1]
        assert n_seq % steps == 0
        per_step = n_seq // steps
        for sq in range(per_step):
            for p in range(n_pages):
                in_specs.append(pl.BlockSpec(
                    (1, rows, HEAD_W),
                    lambda r, pt, sq=sq, p=p: (pt[first_seq + r * per_step + sq, p], 0, 0)))
                args.append(cache)
        n_page_refs = per_step * n_pages
        out_specs.append(pl.BlockSpec((per_step, n_pages * rows, HEAD_W),
                                      lambda r, pt: (r, 0, 0)))
        out_shape.append(jax.ShapeDtypeStruct((n_seq, n_pages * rows, HEAD_W), BF16))
        args = [page_table] + args
    grid_spec = pltpu.PrefetchScalarGridSpec(
        num_scalar_prefetch=0 if pages is None else 1,
        grid=(steps,),
        in_specs=in_specs,
        out_specs=out_specs,
        scratch_shapes=[pltpu.VMEM((tm, d), BF16), pltpu.VMEM((tm, f), BF16)],
    )
    return pl.pallas_call(
        functools.partial(_ffn_kernel, final_norm=final_gain is not None,
                          attn_in=attn is not None, q_scale=q_scale,
                          n_page_refs=n_page_refs, n_pages=n_pages),
        grid_spec=grid_spec,
        out_shape=out_shape,
        compiler_params=_params(1),
        name="ffn",
    )(*args)


def _glu_kernel(x_ref, g_ref, w_ref, b_ref, u_ref):
    d = u_ref.shape[1]
    xn = _rms(x_ref[...], g_ref[...]).astype(BF16)
    y = jnp.dot(xn, w_ref[...], preferred_element_type=F32) + b_ref[...]
    u_ref[...] = y[:, :d] * jax.nn.sigmoid(y[:, d:])


def _glu_proj(x, gains, w, b, l):
    t, d = x.shape
    tm = min(PROJ_TILE, t)
    return pl.pallas_call(
        _glu_kernel,
        grid=(t // tm,),
        in_specs=[
            pl.BlockSpec((tm, d), lambda r: (r, 0)),
            _resident((None, 1, d), lambda r: (l, 0, 0)),
            _resident((None, d, 2 * d), lambda r: (l, 0, 0)),
            _resident((None, 1, 2 * d), lambda r: (l, 0, 0)),
        ],
        out_specs=pl.BlockSpec((tm, d), lambda r: (r, 0)),
        out_shape=jax.ShapeDtypeStruct((t, d), F32),
        compiler_params=_params(1),
        name="conv_glu",
    )(x, gains, w, b)


def _rope_head(xh, cos, sin_signed):
    half = HEAD_DIM // 2
    lane = lax.broadcasted_iota(jnp.int32, xh.shape, 1)
    partner = jnp.where(lane % HEAD_DIM < half,
                        pltpu.roll(xh, HEAD_W - half, 1),
                        pltpu.roll(xh, half, 1))
    return xh * cos + partner * sin_signed


def _kv_kernel(x_ref, g_ref, w_ref, cos_ref, sin_ref, k_ref, v_ref, *attn_refs):
    d = k_ref.shape[1]
    xn = _rms(x_ref[...], g_ref[...]).astype(BF16)
    y = jnp.dot(xn, w_ref[...], preferred_element_type=F32)
    v_ref[...] = y[:, d:]
    cos = cos_ref[...]
    sin = sin_ref[...]
    for h in range(d // HEAD_W):
        sl = slice(h * HEAD_W, (h + 1) * HEAD_W)
        kh = _rope_head(y[:, sl], cos, sin)
        k_ref[:, sl] = kh
        if attn_refs:
            kb_ref, vt_ref = attn_refs
            kb_ref[:, sl] = kh.astype(BF16)
            vt_ref[h] = y[:, d + h * HEAD_W:d + (h + 1) * HEAD_W].T.astype(BF16)


def _kv_proj(x, gain, w, cos, sin, pos_tiles, attn_layout):
    t, d = x.shape
    tm = min(TOKEN_TILE, t)
    tok = pl.BlockSpec((tm, d), lambda r: (r, 0))
    pos = pl.BlockSpec((tm, HEAD_W), lambda r: (r % pos_tiles, 0))
    out_specs = [tok, tok]
    out_shape = [jax.ShapeDtypeStruct((t, d), F32), jax.ShapeDtypeStruct((t, d), F32)]
    if attn_layout:
        n_heads = d // HEAD_W
        out_specs += [tok, pl.BlockSpec((None, n_heads, None, HEAD_W, tm),
                                        lambda r: (r // pos_tiles, 0, r % pos_tiles, 0, 0))]
        out_shape += [jax.ShapeDtypeStruct((t, d), BF16),
                      jax.ShapeDtypeStruct((t // (pos_tiles * tm), n_heads, pos_tiles, HEAD_W, tm),
                                           BF16)]
    return pl.pallas_call(
        _kv_kernel,
        grid=(t // tm,),
        in_specs=[tok, _resident((1, d), lambda r: (0, 0)),
                  _resident((d, 2 * d), lambda r: (0, 0)), pos, pos],
        out_specs=out_specs,
        out_shape=out_shape,
        compiler_params=_params(1),
        name="kv_proj",
    )(x, gain, w, cos, sin)


def _conv_tail(c, x, lng_ref, lnb_ref, w_ref, b_ref):
    mu = jnp.mean(c, axis=-1, keepdims=True)
    xc = c - mu
    y = xc * lax.rsqrt(jnp.mean(xc * xc, axis=-1, keepdims=True) + EPS)
    y = y * lng_ref[...] + lnb_ref[...]
    z = (y * jax.nn.sigmoid(y)).astype(BF16)
    return x + jnp.dot(z, w_ref[...], preferred_element_type=F32) + b_ref[...]


def _conv_seq_kernel(x_ref, g_ref, wi_ref, bi_ref, dw_ref, dwb_ref, lng_ref, lnb_ref, w_ref, b_ref,
                     o_ref, st_ref, full_ref, c_ref):
    ts, d = c_ref.shape
    first = pl.program_id(1) == 0

    @pl.when(first)
    def _():
        full_ref[0:CONV_HALO, :] = jnp.zeros((CONV_HALO, d), F32)

    @pl.when(jnp.logical_not(first))
    def _():
        full_ref[0:CONV_HALO, :] = full_ref[ts:ts + CONV_HALO, :]

    xn = _rms(x_ref[0], g_ref[...]).astype(BF16)
    y = jnp.dot(xn, wi_ref[...], preferred_element_type=F32) + bi_ref[...]
    full_ref[CONV_HALO:CONV_HALO + ts, :] = y[:, :d] * jax.nn.sigmoid(y[:, d:])
    full_ref[CONV_HALO + ts:, :] = jnp.zeros((CONV_TAIL, d), F32)

    @pl.when(pl.program_id(1) == pl.num_programs(1) - 1)
    def _():
        hist = st_ref.shape[1]
        st_ref[0] = full_ref[CONV_HALO + ts - hist:CONV_HALO + ts, :]
    lead = CONV_HALO - (CONV_WIDTH - 1)

    n_a = (lead + CONV_WIDTH - 1) // CONV_ROWS + 1
    sub = lax.broadcasted_iota(jnp.int32, (CONV_ROWS, HEAD_W), 0)

    for cb in range(d // HEAD_W):
        sl = slice(cb * HEAD_W, (cb + 1) * HEAD_W)
        taps = {}
        for a in range(n_a):
            for b in range(CONV_ROWS):
                j = CONV_ROWS * a + b - lead
                if 0 <= j < CONV_WIDTH:
                    taps[a, b] = jnp.broadcast_to(dw_ref[j:j + 1, sl], (CONV_ROWS, HEAD_W))
        bias = jnp.broadcast_to(dwb_ref[:, sl], (CONV_ROWS, HEAD_W))

        def partials(r):
            base = pl.multiple_of(r * CONV_ROWS, CONV_ROWS)
            rows = [full_ref[pl.ds(base + CONV_ROWS * a, CONV_ROWS), sl] for a in range(n_a)]
            out = []
            for b in range(CONV_ROWS):
                z = None
                for a in range(n_a):
                    if (a, b) in taps:
                        term = rows[a] * taps[a, b]
                        z = term if z is None else z + term
                out.append(z)
            return tuple(out)

        def block(r, z_prev):
            z_next = partials(r)
            acc = bias + z_prev[0]
            for b in range(1, CONV_ROWS):
                mixed = jnp.where(sub >= b, z_prev[b], z_next[b])
                acc = acc + pltpu.roll(mixed, CONV_ROWS - b, 0)
            c_ref[pl.ds(pl.multiple_of((r - 1) * CONV_ROWS, CONV_ROWS), CONV_ROWS), sl] = acc
            return z_next

        lax.fori_loop(1, ts // CONV_ROWS + 1, block, partials(0), unroll=2)

    o_ref[0] = _conv_tail(c_ref[...], x_ref[0], lng_ref, lnb_ref, w_ref, b_ref)


def _conv_seq(x, gains, wi, bi, dw, dwb, lng, lnb, w, b, l):
    bsz, s, d = x.shape
    hist = CONV_WIDTH - 1
    ts = min(TOKEN_TILE, s)
    assert s % ts == 0 and ts >= CONV_HALO
    tile = pl.BlockSpec((1, ts, d), lambda bi, i: (bi, i, 0))
    vec = _resident((None, 1, d), lambda bi, i: (l, 0, 0))
    return pl.pallas_call(
        _conv_seq_kernel,
        grid=(bsz, s // ts),
        in_specs=[
            tile,
            vec,
            _resident((None, d, 2 * d), lambda bi, i: (l, 0, 0)),
            _resident((None, 1, 2 * d), lambda bi, i: (l, 0, 0)),
            _resident((None, CONV_WIDTH, d), lambda bi, i: (l, 0, 0)),
            vec, vec, vec,
            _resident((None, d, d), lambda bi, i: (l, 0, 0)),
            vec,
        ],
        out_specs=[tile, pl.BlockSpec((1, hist, d), lambda bi, i: (bi, 0, 0))],
        out_shape=[jax.ShapeDtypeStruct((bsz, s, d), F32),
                   jax.ShapeDtypeStruct((bsz, hist, d), F32)],
        scratch_shapes=[pltpu.VMEM((ts + CONV_HALO + CONV_TAIL, d), F32), pltpu.VMEM((ts, d), F32)],
        compiler_params=_params(2),
        name="conv_seq",
    )(x, gains, wi, bi, dw, dwb, lng, lnb, w, b)


def _conv_step_kernel(st_ref, u_ref, x_ref, dw_ref, dwb_ref, lng_ref, lnb_ref, w_ref, b_ref,
                      o_ref, ns_ref):
    hist = CONV_WIDTH - 1
    u = u_ref[...]
    c = u * dw_ref[hist:hist + 1, :] + dwb_ref[...]
    for j in range(hist):
        c = c + st_ref[j] * dw_ref[j:j + 1, :]
    ns_ref[0:hist - 1] = st_ref[1:hist]
    ns_ref[hist - 1] = u
    o_ref[...] = _conv_tail(c, x_ref[...], lng_ref, lnb_ref, w_ref, b_ref)


def _conv_step(state, u, x, dw, dwb, lng, lnb, w, b, l):
    n, d = u.shape
    nb = min(16, n)
    assert n % nb == 0
    hist = CONV_WIDTH - 1
    tok = pl.BlockSpec((nb, d), lambda r: (r, 0))
    vec = _resident((None, 1, d), lambda r: (l, 0, 0))
    return pl.pallas_call(
        _conv_step_kernel,
        grid=(n // nb,),
        in_specs=[
            pl.BlockSpec((None, hist, nb, d), lambda r: (l, 0, r, 0)),
            tok, tok,
            _resident((None, CONV_WIDTH, d), lambda r: (l, 0, 0)),
            vec, vec, vec,
            _resident((None, d, d), lambda r: (l, 0, 0)),
            vec,
        ],
        out_specs=[tok, pl.BlockSpec((hist, nb, d), lambda r: (0, r, 0))],
        out_shape=[jax.ShapeDtypeStruct((n, d), F32), jax.ShapeDtypeStruct((hist, n, d), F32)],
        compiler_params=_params(1),
        name="conv_step",
    )(state, u, x, dw, dwb, lng, lnb, w, b)


def _lambda(lam_ref, lam_init):
    lv = lam_ref[...]
    a = jnp.sum(lv[0:1] * lv[1:2], axis=-1, keepdims=True)
    b = jnp.sum(lv[2:3] * lv[3:4], axis=-1, keepdims=True)
    return jnp.exp(a) - jnp.exp(b) + lam_init


def _sub_norm(o, g, lam_init):
    return o * lax.rsqrt(jnp.mean(o * o, axis=-1, keepdims=True) + EPS) * g * (1.0 - lam_init)


def _flash_kernel(lam_ref, g_ref, q_ref, k_ref, vt_ref, o_ref, qq_ref, acc_ref, *, lam_init):
    tq = q_ref.shape[1]
    tk = vt_ref.shape[2]
    i = pl.program_id(2)
    q = q_ref[0]
    lane = lax.broadcasted_iota(jnp.int32, q.shape, 1)
    zero = jnp.zeros_like(q)
    qq_ref[0] = jnp.where(lane < HEAD_DIM, q, zero)
    qq_ref[1] = jnp.where(lane >= HEAD_DIM, q, zero)
    acc_ref[...] = jnp.zeros(acc_ref.shape, F32)

    def steps(js, carry, mask_last):
        half = tq // 2
        pieces = []
        for n, j in enumerate(js):
            if mask_last and n == len(js) - 1:
                pieces += [(j, slice(0, half), slice(0, half), True),
                           (j, slice(half, tq), slice(0, tk), True)]
            else:
                pieces.append((j, slice(0, tq), slice(0, tk), False))
        ss = []
        for j, cols, keys, _ in pieces:
            kc = k_ref[0, pl.ds(pl.multiple_of(j * tk, tk) + keys.start, keys.stop - keys.start), :]
            ss.append([lax.dot_general(kc, qq_ref[c, cols, :], (((1,), (1,)), ((), ())),
                                       preferred_element_type=F32) for c in range(2)])
        carry = [list(st) for st in carry]
        work = []
        for (j, cols, keys, masked), s_pair in zip(pieces, ss):
            for c in range(2):
                m_old, l_old = carry[c][0][:, cols], carry[c][1][:, cols]
                s = s_pair[c]
                if masked:
                    kpos = keys.start + lax.broadcasted_iota(jnp.int32, s.shape, 0)
                    qpos = cols.start + lax.broadcasted_iota(jnp.int32, s.shape, 1)
                    s = jnp.where(kpos <= qpos, s, NEG_BIG)
                m_new = jnp.maximum(m_old, jnp.max(s, axis=0, keepdims=True))
                alpha = jnp.exp2(m_old - m_new)
                p = jnp.exp2(s - m_new)
                l_new = alpha * l_old + jnp.sum(p, axis=0, keepdims=True)
                for idx, new in ((0, m_new), (1, l_new)):
                    old = carry[c][idx]
                    parts = ([old[:, :cols.start]] if cols.start else []) + [new]
                    parts += [old[:, cols.stop:]] if cols.stop < tq else []
                    carry[c][idx] = jnp.concatenate(parts, axis=1)
                work.append((c, j, cols, keys, alpha, p.astype(BF16)))
        for c, j, cols, keys, alpha, p in work:
            acc_ref[c, :, cols] = alpha * acc_ref[c, :, cols] + jnp.dot(
                vt_ref[j, :, keys], p, preferred_element_type=F32)
        return tuple(tuple(st) for st in carry)

    stat = (jnp.full((1, tq), NEG_BIG, F32), jnp.zeros((1, tq), F32))
    carry = lax.fori_loop(0, i // 2, lambda jj, c: steps([2 * jj, 2 * jj + 1], c, False),
                          (stat, stat))
    (_, l1), (_, l2) = lax.cond(i % 2 == 1,
                                lambda c: steps([i - 1, i], c, True),
                                lambda c: steps([i], c, True), carry)

    ot = acc_ref[0] * (1.0 / l1) - _lambda(lam_ref, lam_init) * (acc_ref[1] * (1.0 / l2))
    ot = ot * lax.rsqrt(jnp.mean(ot * ot, axis=0, keepdims=True) + EPS)
    ot = ot * (g_ref[...] * (1.0 - lam_init))
    o_ref[0] = ot.T.astype(o_ref.dtype)


def _flash_attn(q, k, vt, lam, g_col, j, lam_init):
    bsz, s, d = q.shape
    tile = vt.shape[4]
    assert s % tile == 0 and vt.shape[2] == s // tile
    q_spec = pl.BlockSpec((1, tile, HEAD_W), lambda bi, h, i: (bi, i, h))
    return pl.pallas_call(
        functools.partial(_flash_kernel, lam_init=lam_init),
        grid=(bsz, d // HEAD_W, s // tile),
        in_specs=[
            pl.BlockSpec((None, 4, HEAD_DIM), lambda bi, h, i: (j, 0, 0)),
            pl.BlockSpec((None, HEAD_W, 1), lambda bi, h, i: (j, 0, 0)),
            q_spec,
            pl.BlockSpec((1, s, HEAD_W), lambda bi, h, i: (bi, 0, h)),
            pl.BlockSpec((None, None, s // tile, HEAD_W, tile), lambda bi, h, i: (bi, h, 0, 0, 0)),
        ],
        out_specs=q_spec,
        out_shape=jax.ShapeDtypeStruct((bsz, s, d), BF16),
        scratch_shapes=[pltpu.VMEM((2, tile, HEAD_W), BF16), pltpu.VMEM((2, HEAD_W, tile), F32)],
        compiler_params=_params(3),
        name="flash_diff_attn",
    )(lam, g_col, q, k, vt)


def _decode_kernel(lam_ref, g_ref, q_ref, kn_ref, vn_ref, k_ref, v_ref, o_ref, *, lam_init):
    n_heads = q_ref.shape[1]
    shape = (2 * n_heads, HEAD_W)
    own_half = (lax.broadcasted_iota(jnp.int32, shape, 1) // HEAD_DIM ==
                lax.broadcasted_iota(jnp.int32, shape, 0) // n_heads)
    lam = _lambda(lam_ref, lam_init)
    for sq in range(q_ref.shape[0]):
        q = q_ref[sq].astype(F32)
        qrows = jnp.where(own_half, jnp.concatenate([q, q], axis=0), 0.0)
        s = lax.dot_general(qrows.astype(BF16), k_ref[sq], (((1,), (1,)), ((), ())),
                            preferred_element_type=F32)
        own = (lax.broadcasted_iota(jnp.int32, s.shape, 1) % n_heads ==
               lax.broadcasted_iota(jnp.int32, s.shape, 0) % n_heads)
        s = jnp.where(own, s, NEG_BIG)
        kn = kn_ref[sq]
        vn = vn_ref[sq]
        s_new = jnp.sum(qrows * jnp.concatenate([kn, kn], axis=0), axis=1, keepdims=True)
        m = jnp.maximum(jnp.max(s, axis=1, keepdims=True), s_new)
        p = jnp.exp2(s - m)
        p_new = jnp.exp2(s_new - m)
        inv_l = 1.0 / (jnp.sum(p, axis=1, keepdims=True) + p_new)
        o = jnp.dot(p.astype(BF16), v_ref[sq], preferred_element_type=F32)
        o = (o + p_new * jnp.concatenate([vn, vn], axis=0)) * inv_l
        res = o[:n_heads] - lam * o[n_heads:]
        o_ref[sq] = _sub_norm(res, g_ref[...], lam_init).astype(o_ref.dtype)


def _decode_attn(q, k_new, v_new, k_slab, v_slab, first_seq, lam, g, j, lam_init):
    n, rows, _ = k_slab.shape
    n_heads = q.shape[1]
    per = DECODE_SEQS if n % DECODE_SEQS == 0 else 1
    assert first_seq % per == 0
    row = pl.BlockSpec((per, n_heads, HEAD_W), lambda b: (first_seq // per + b, 0, 0))
    slab = pl.BlockSpec((per, rows, HEAD_W), lambda b: (b, 0, 0))
    return pl.pallas_call(
        functools.partial(_decode_kernel, lam_init=lam_init),
        grid=(n // per,),
        in_specs=[
            pl.BlockSpec((None, 4, HEAD_DIM), lambda b: (j, 0, 0)),
            pl.BlockSpec((None, 1, HEAD_W), lambda b: (j, 0, 0)),
            row, row, row, slab, slab,
        ],
        out_specs=pl.BlockSpec((per, n_heads, HEAD_W), lambda b: (b, 0, 0)),
        out_shape=jax.ShapeDtypeStruct((n, n_heads, HEAD_W), BF16),
        compiler_params=_params(1),
        name="decode_attn",
    )(lam, g, q, k_new, v_new, k_slab, v_slab)


def _rope_tables(pos):
    half = HEAD_DIM // 2
    freqs = ROPE_THETA ** (-jnp.arange(half, dtype=F32) * 2.0 / HEAD_DIM)
    ang = pos.astype(F32)[:, None] * freqs[None, :]
    cos, sin = jnp.cos(ang), jnp.sin(ang)
    return jnp.tile(cos, (1, 4)), jnp.tile(jnp.concatenate([-sin, sin], axis=1), (1, 2))


def kernel(x_prompt, x_sample, state_conv, cache_k, cache_v, page_table, norm_ffn, ffn_w_in, ffn_w_out, norm_mix, conv_w_in, conv_b_in, conv_dw, conv_dw_b, conv_ln_g, conv_ln_b, conv_w_out, conv_b_out, kv_norm, w_kv, w_q, lambda_q1, lambda_k1, lambda_q2, lambda_k2, subln_g, w_o, norm_final):
    bsz, s, d = x_prompt.shape
    n_dec, dec_seq, _ = x_sample.shape
    assert dec_seq == 1
    depth = norm_mix.shape[0]
    n_a = conv_w_in.shape[0]
    n_heads = d // HEAD_W
    past_len = page_table.shape[1] * cache_k.shape[1]

    w_in = ffn_w_in.astype(BF16)
    w_out = ffn_w_out.astype(BF16)
    cw_in = conv_w_in.astype(BF16)
    cw_out = conv_w_out.astype(BF16)
    wkv = w_kv.astype(BF16)
    wq = w_q.astype(BF16)
    wo = w_o.astype(BF16)
    g_ffn = norm_ffn[:, :, None, :]
    g_mix = norm_mix[:, None, :]
    g_kv = kv_norm[None, :]
    g_fin = norm_final[None, :]
    cb_in = conv_b_in[:, None, :]
    cdw_b = conv_dw_b[:, None, :]
    cln_g = conv_ln_g[:, None, :]
    cln_b = conv_ln_b[:, None, :]
    cb_out = conv_b_out[:, None, :]
    lam = jnp.stack([lambda_q1, lambda_k1, lambda_q2, lambda_k2], axis=1)
    g_sub = subln_g[:, None, :]
    g_sub_col = subln_g[:, :, None]

    q_scale = HEAD_DIM ** -0.5 * math.log2(math.e)

    def run(x, cos, sin, pos_tiles, ffn_fn, conv_fn, attn_fn, attn_layout):
        conv_rows = []
        k = v = kv_ctx = None
        for l in range(depth):
            if l == n_a:
                kv_ctx = _kv_proj(x, g_kv, wkv, cos, sin, pos_tiles, attn_layout)
                k, v = kv_ctx[:2]
            if l < n_a:
                x = ffn_fn(x, l, 0)[0]
                x, state = conv_fn(x, l)
                conv_rows.append(state)
                x = ffn_fn(x, l, 1)[0]
            else:
                j = l - n_a
                lam_init = 0.8 - 0.6 * math.exp(-0.3 * l)
                x, q = ffn_fn(x, l, 0, query=(g_mix, wq, j, cos, sin, q_scale))[:2]
                o = attn_fn(q, kv_ctx, j, lam_init)
                x = ffn_fn(x, l, 1, attn=(o, wo, j))[0]
        return x, conv_rows, k, v

    def final_gain(l, i):
        return g_fin if (l == depth - 1 and i == 1) else None

    cos_p, sin_p = _rope_tables(jnp.arange(s, dtype=jnp.int32))
    tiles_p = s // min(TOKEN_TILE, s)

    n_groups = depth
    group = n_dec // n_groups
    assert n_dec % n_groups == 0
    pool, page = cache_k.shape[:2]
    caches = (cache_k.reshape(pool, page * n_heads, HEAD_W),
              cache_v.reshape(pool, page * n_heads, HEAD_W))
    slabs = ([], [])

    def ffn_prompt(x, l, i, **fused):
        call = 2 * l + i
        which, grp = divmod(call, n_groups)
        *outs, slab = _ffn(x, g_ffn, w_in, w_out, l, i, final_gain(l, i),
                           pages=(page_table, caches[which], grp * group, group), **fused)
        slabs[which].append(slab)
        return outs

    def conv_prompt(x, l):
        y, state = _conv_seq(x.reshape(bsz, s, d), g_mix, cw_in, cb_in, conv_dw, cdw_b, cln_g,
                             cln_b, cw_out, cb_out, l)
        return y.reshape(bsz * s, d), state

    def attn_prompt(q, kv_ctx, j, lam_init):
        _, _, kb, vt = kv_ctx
        o = _flash_attn(q.reshape(bsz, s, d), kb.reshape(bsz, s, d), vt, lam, g_sub_col, j,
                        lam_init)
        return o.reshape(bsz * s, d)

    y_p, conv_p, k_p, v_p = run(x_prompt.reshape(bsz * s, d), cos_p, sin_p, tiles_p,
                                ffn_prompt, conv_prompt, attn_prompt, True)
    conv_p = jnp.stack(conv_p, axis=0)

    cos_s, sin_s = _rope_tables(jnp.full((n_dec,), past_len, dtype=jnp.int32))

    state_hist = state_conv.transpose(0, 2, 1, 3)

    def conv_sample(x, l):
        u = _glu_proj(x, g_mix, cw_in, cb_in, l)
        return _conv_step(state_hist, u, x, conv_dw, cdw_b, cln_g, cln_b, cw_out, cb_out, l)

    def ffn_sample(x, l, i, **fused):
        return _ffn(x, g_ffn, w_in, w_out, l, i, final_gain(l, i), **fused)

    def attn_sample(q, kv_ctx, j, lam_init):
        k, v = kv_ctx
        per_head = (n_dec, n_heads, HEAD_W)
        rows = [_decode_attn(q.reshape(per_head), k.reshape(per_head), v.reshape(per_head),
                             slabs[0][grp], slabs[1][grp], grp * group, lam, g_sub, j, lam_init)
                for grp in range(n_groups)]
        return jnp.concatenate(rows, axis=0).reshape(n_dec, d)

    y_s, conv_s, k_s, v_s = run(x_sample.reshape(n_dec, d), cos_s, sin_s, 1,
                                ffn_sample, conv_sample, attn_sample, False)
    conv_s = jnp.stack(conv_s, axis=0).transpose(0, 2, 1, 3)

    return (y_p.reshape(bsz, s, d), y_s.reshape(n_dec, 1, d), conv_p,
            k_p.reshape(bsz, s, n_heads, HEAD_W), v_p.reshape(bsz, s, n_heads, HEAD_W),
            conv_s, k_s.reshape(n_dec, 1, n_heads, HEAD_W), v_s.reshape(n_dec, 1, n_heads, HEAD_W))
```

```python
import functools
import math

import jax
import jax.numpy as jnp
from jax import lax
from jax.experimental import pallas as pl
from jax.experimental.pallas import tpu as pltpu

HEAD_DIM = 64
HEAD_W = 2 * HEAD_DIM
CONV_WIDTH = 31
CONV_HALO = 32
ROPE_THETA = 10000.0
EPS = 1e-6
NEG_BIG = -1e30

TOKEN_TILE = 512
PROJ_TILE = 1024
FFN_CHUNK = 256
DECODE_SEQS = 2
CONV_ROWS = 8
CONV_TAIL = CONV_ROWS
VMEM_LIMIT = 60 * 1024 * 1024

F32 = jnp.float32
BF16 = jnp.bfloat16


def _params(n_axes, vmem=VMEM_LIMIT):
    return pltpu.CompilerParams(dimension_semantics=("arbitrary",) * n_axes,
                                vmem_limit_bytes=vmem)


def _rms(x, g):
    return x * lax.rsqrt(jnp.mean(x * x, axis=-1, keepdims=True) + EPS) * g


def _resident(shape, index_map):
    return pl.BlockSpec(shape, index_map, pipeline_mode=pl.Buffered(1))


def _compact_pages(page_refs, out_ref, n_pages):
    for idx, ref in enumerate(page_refs):
        sidx, p = divmod(idx, n_pages)
        rows = ref.shape[1]
        out_ref[sidx, p * rows:(p + 1) * rows, :] = ref[0].astype(BF16)


def _ffn_kernel(*refs, final_norm, attn_in, q_scale, n_page_refs, n_pages):
    refs = list(refs)
    if n_page_refs:
        refs.pop(0)
    x_ref, g_ref, wg_ref, wu_ref, wo_ref = refs[:5]
    del refs[:5]
    if final_norm:
        gf_ref = refs.pop(0)
    if attn_in:
        a_ref, wa_ref = refs[:2]
        del refs[:2]
    if q_scale is not None:
        gq_ref, wq_ref, cos_ref, sin_ref = refs[:4]
        del refs[:4]
    page_refs = refs[:n_page_refs]
    del refs[:n_page_refs]
    o_ref = refs.pop(0)
    if q_scale is not None:
        q_ref = refs.pop(0)
    if n_page_refs:
        _compact_pages(page_refs, refs.pop(0), n_pages)
    xn_ref, h_ref = refs
    x = x_ref[...]
    if attn_in:
        x = x + jnp.dot(a_ref[...], wa_ref[...], preferred_element_type=F32)
    xn_ref[...] = _rms(x, g_ref[...]).astype(BF16)
    d_ff = h_ref.shape[1]
    for c in range(0, d_ff, FFN_CHUNK):
        xn = xn_ref[...]
        gate = jnp.dot(xn, wg_ref[:, c:c + FFN_CHUNK], preferred_element_type=F32)
        up = jnp.dot(xn, wu_ref[:, c:c + FFN_CHUNK], preferred_element_type=F32)
        h_ref[:, c:c + FFN_CHUNK] = (gate * jax.nn.sigmoid(gate) * up).astype(BF16)
    y = x + 0.5 * jnp.dot(h_ref[...], wo_ref[...], preferred_element_type=F32)
    if final_norm:
        y = _rms(y, gf_ref[...])
    o_ref[...] = y
    if q_scale is not None:
        yq = jnp.dot(_rms(y, gq_ref[...]).astype(BF16), wq_ref[...], preferred_element_type=F32)
        cos = cos_ref[...]
        sin = sin_ref[...]
        for h in range(yq.shape[1] // HEAD_W):
            sl = slice(h * HEAD_W, (h + 1) * HEAD_W)
            q_ref[:, sl] = (_rope_head(yq[:, sl], cos, sin) * q_scale).astype(q_ref.dtype)


def _ffn(x, gains, w_in, w_out, l, i, final_gain=None, pages=None, attn=None, query=None):
    t, d = x.shape
    f = w_out.shape[2]
    assert f % FFN_CHUNK == 0
    tm = min(TOKEN_TILE, t)
    assert t % tm == 0
    steps = t // tm
    tok = pl.BlockSpec((tm, d), lambda r, *_: (r, 0))
    in_specs = [
        tok,
        _resident((None, None, 1, d), lambda r, *_: (l, i, 0, 0)),
        _resident((None, None, d, f), lambda r, *_: (l, i, 0, 0)),
        _resident((None, None, d, f), lambda r, *_: (l, i, 0, 1)),
        _resident((None, None, f, d), lambda r, *_: (l, i, 0, 0)),
    ]
    args = [x, gains, w_in, w_in, w_out]
    out_specs = [tok]
    out_shape = [jax.ShapeDtypeStruct((t, d), F32)]
    if final_gain is not None:
        in_specs.append(_resident((1, d), lambda r, *_: (0, 0)))
        args.append(final_gain)
    if attn is not None:
        a, wa, ja = attn
        in_specs += [tok, _resident((None, d, d), lambda r, *_: (ja, 0, 0))]
        args += [a, wa]
    q_scale = None
    if query is not None:
        gq, wq, jq, cos, sin, q_scale = query
        pos_tiles = cos.shape[0] // tm
        pos = pl.BlockSpec((tm, HEAD_W), lambda r, *_: (r % pos_tiles, 0))
        in_specs += [_resident((None, 1, d), lambda r, *_: (l, 0, 0)),
                     _resident((None, d, d), lambda r, *_: (jq, 0, 0)), pos, pos]
        args += [gq, wq, cos, sin]
        out_specs.append(tok)
        out_shape.append(jax.ShapeDtypeStruct((t, d), BF16))
    n_page_refs = n_pages = 0
    if pages is not None:
        page_table, cache, first_seq, n_seq = pages
        n_pages = page_table.shape[1]
        rows = cache.shape[1]
        assert n_seq % steps == 0
        per_step = n_seq // steps
        for sq in range(per_step):
            for p in range(n_pages):
                in_specs.append(pl.BlockSpec(
                    (1, rows, HEAD_W),
                    lambda r, pt, sq=sq, p=p: (pt[first_seq + r * per_step + sq, p], 0, 0)))
                args.append(cache)
        n_page_refs = per_step * n_pages
        out_specs.append(pl.BlockSpec((per_step, n_pages * rows, HEAD_W),
                                      lambda r, pt: (r, 0, 0)))
        out_shape.append(jax.ShapeDtypeStruct((n_seq, n_pages * rows, HEAD_W), BF16))
        args = [page_table] + args
    grid_spec = pltpu.PrefetchScalarGridSpec(
        num_scalar_prefetch=0 if pages is None else 1,
        grid=(steps,),
        in_specs=in_specs,
        out_specs=out_specs,
        scratch_shapes=[pltpu.VMEM((tm, d), BF16), pltpu.VMEM((tm, f), BF16)],
    )
    return pl.pallas_call(
        functools.partial(_ffn_kernel, final_norm=final_gain is not None,
                          attn_in=attn is not None, q_scale=q_scale,
                          n_page_refs=n_page_refs, n_pages=n_pages),
        grid_spec=grid_spec,
        out_shape=out_shape,
        compiler_params=_params(1),
        name="ffn",
    )(*args)


def _glu_kernel(x_ref, g_ref, w_ref, b_ref, u_ref):
    d = u_ref.shape[1]
    xn = _rms(x_ref[...], g_ref[...]).astype(BF16)
    y = jnp.dot(xn, w_ref[...], preferred_element_type=F32) + b_ref[...]
    u_ref[...] = y[:, :d] * jax.nn.sigmoid(y[:, d:])


def _glu_proj(x, gains, w, b, l):
    t, d = x.shape
    tm = min(PROJ_TILE, t)
    return pl.pallas_call(
        _glu_kernel,
        grid=(t // tm,),
        in_specs=[
            pl.BlockSpec((tm, d), lambda r: (r, 0)),
            _resident((None, 1, d), lambda r: (l, 0, 0)),
            _resident((None, d, 2 * d), lambda r: (l, 0, 0)),
            _resident((None, 1, 2 * d), lambda r: (l, 0, 0)),
        ],
        out_specs=pl.BlockSpec((tm, d), lambda r: (r, 0)),
        out_shape=jax.ShapeDtypeStruct((t, d), F32),
        compiler_params=_params(1),
        name="conv_glu",
    )(x, gains, w, b)


def _rope_head(xh, cos, sin_signed):
    half = HEAD_DIM // 2
    lane = lax.broadcasted_iota(jnp.int32, xh.shape, 1)
    partner = jnp.where(lane % HEAD_DIM < half,
                        pltpu.roll(xh, HEAD_W - half, 1),
                        pltpu.roll(xh, half, 1))
    return xh * cos + partner * sin_signed


def _kv_kernel(x_ref, g_ref, w_ref, cos_ref, sin_ref, k_ref, v_ref, *attn_refs):
    d = k_ref.shape[1]
    xn = _rms(x_ref[...], g_ref[...]).astype(BF16)
    y = jnp.dot(xn, w_ref[...], preferred_element_type=F32)
    v_ref[...] = y[:, d:]
    cos = cos_ref[...]
    sin = sin_ref[...]
    for h in range(d // HEAD_W):
        sl = slice(h * HEAD_W, (h + 1) * HEAD_W)
        kh = _rope_head(y[:, sl], cos, sin)
        k_ref[:, sl] = kh
        if attn_refs:
            kb_ref, vt_ref = attn_refs
            kb_ref[:, sl] = kh.astype(BF16)
            vt_ref[h] = y[:, d + h * HEAD_W:d + (h + 1) * HEAD_W].T.astype(BF16)


def _kv_proj(x, gain, w, cos, sin, pos_tiles, attn_layout):
    t, d = x.shape
    tm = min(TOKEN_TILE, t)
    tok = pl.BlockSpec((tm, d), lambda r: (r, 0))
    pos = pl.BlockSpec((tm, HEAD_W), lambda r: (r % pos_tiles, 0))
    out_specs = [tok, tok]
    out_shape = [jax.ShapeDtypeStruct((t, d), F32), jax.ShapeDtypeStruct((t, d), F32)]
    if attn_layout:
        n_heads = d // HEAD_W
        out_specs += [tok, pl.BlockSpec((None, n_heads, None, HEAD_W, tm),
                                        lambda r: (r // pos_tiles, 0, r % pos_tiles, 0, 0))]
        out_shape += [jax.ShapeDtypeStruct((t, d), BF16),
                      jax.ShapeDtypeStruct((t // (pos_tiles * tm), n_heads, pos_tiles, HEAD_W, tm),
                                           BF16)]
    return pl.pallas_call(
        _kv_kernel,
        grid=(t // tm,),
        in_specs=[tok, _resident((1, d), lambda r: (0, 0)),
                  _resident((d, 2 * d), lambda r: (0, 0)), pos, pos],
        out_specs=out_specs,
        out_shape=out_shape,
        compiler_params=_params(1),
        name="kv_proj",
    )(x, gain, w, cos, sin)


def _conv_tail(c, x, lng_ref, lnb_ref, w_ref, b_ref):
    mu = jnp.mean(c, axis=-1, keepdims=True)
    xc = c - mu
    y = xc * lax.rsqrt(jnp.mean(xc * xc, axis=-1, keepdims=True) + EPS)
    y = y * lng_ref[...] + lnb_ref[...]
    z = (y * jax.nn.sigmoid(y)).astype(BF16)
    return x + jnp.dot(z, w_ref[...], preferred_element_type=F32) + b_ref[...]


def _conv_seq_kernel(x_ref, g_ref, wi_ref, bi_ref, dw_ref, dwb_ref, lng_ref, lnb_ref, w_ref, b_ref,
                     o_ref, st_ref, full_ref, c_ref):
    ts, d = c_ref.shape
    first = pl.program_id(1) == 0

    @pl.when(first)
    def _():
        full_ref[0:CONV_HALO, :] = jnp.zeros((CONV_HALO, d), F32)

    @pl.when(jnp.logical_not(first))
    def _():
        full_ref[0:CONV_HALO, :] = full_ref[ts:ts + CONV_HALO, :]

    xn = _rms(x_ref[0], g_ref[...]).astype(BF16)
    y = jnp.dot(xn, wi_ref[...], preferred_element_type=F32) + bi_ref[...]
    full_ref[CONV_HALO:CONV_HALO + ts, :] = y[:, :d] * jax.nn.sigmoid(y[:, d:])
    full_ref[CONV_HALO + ts:, :] = jnp.zeros((CONV_TAIL, d), F32)

    @pl.when(pl.program_id(1) == pl.num_programs(1) - 1)
    def _():
        hist = st_ref.shape[1]
        st_ref[0] = full_ref[CONV_HALO + ts - hist:CONV_HALO + ts, :]
    lead = CONV_HALO - (CONV_WIDTH - 1)

    n_a = (lead + CONV_WIDTH - 1) // CONV_ROWS + 1
    sub = lax.broadcasted_iota(jnp.int32, (CONV_ROWS, HEAD_W), 0)

    for cb in range(d // HEAD_W):
        sl = slice(cb * HEAD_W, (cb + 1) * HEAD_W)
        taps = {}
        for a in range(n_a):
            for b in range(CONV_ROWS):
                j = CONV_ROWS * a + b - lead
                if 0 <= j < CONV_WIDTH:
                    taps[a, b] = jnp.broadcast_to(dw_ref[j:j + 1, sl], (CONV_ROWS, HEAD_W))
        bias = jnp.broadcast_to(dwb_ref[:, sl], (CONV_ROWS, HEAD_W))

        def partials(r):
            base = r * CONV_ROWS
            rows = [full_ref[pl.ds(base + CONV_ROWS * a, CONV_ROWS), sl] for a in range(n_a)]
            out = []
            for b in range(CONV_ROWS):
                z = None
                for a in range(n_a):
                    if (a, b) in taps:
                        term = rows[a] * taps[a, b]
                        z = term if z is None else z + term
                out.append(z)
            return tuple(out)

        def block(r, z_prev):
            z_next = partials(r)
            acc = bias + z_prev[0]
            for b in range(1, CONV_ROWS):
                mixed = jnp.where(sub >= b, z_prev[b], z_next[b])
                acc = acc + pltpu.roll(mixed, CONV_ROWS - b, 0)
            c_ref[pl.ds((r - 1) * CONV_ROWS, CONV_ROWS), sl] = acc
            return z_next

        z = partials(0)
        for r in range(1, ts // CONV_ROWS + 1):
            z = block(r, z)

    o_ref[0] = _conv_tail(c_ref[...], x_ref[0], lng_ref, lnb_ref, w_ref, b_ref)


def _conv_seq(x, gains, wi, bi, dw, dwb, lng, lnb, w, b, l):
    bsz, s, d = x.shape
    hist = CONV_WIDTH - 1
    ts = min(TOKEN_TILE, s)
    assert s % ts == 0 and ts >= CONV_HALO
    tile = pl.BlockSpec((1, ts, d), lambda bi, i: (bi, i, 0))
    vec = _resident((None, 1, d), lambda bi, i: (l, 0, 0))
    return pl.pallas_call(
        _conv_seq_kernel,
        grid=(bsz, s // ts),
        in_specs=[
            tile,
            vec,
            _resident((None, d, 2 * d), lambda bi, i: (l, 0, 0)),
            _resident((None, 1, 2 * d), lambda bi, i: (l, 0, 0)),
            _resident((None, CONV_WIDTH, d), lambda bi, i: (l, 0, 0)),
            vec, vec, vec,
            _resident((None, d, d), lambda bi, i: (l, 0, 0)),
            vec,
        ],
        out_specs=[tile, pl.BlockSpec((1, hist, d), lambda bi, i: (bi, 0, 0))],
        out_shape=[jax.ShapeDtypeStruct((bsz, s, d), F32),
                   jax.ShapeDtypeStruct((bsz, hist, d), F32)],
        scratch_shapes=[pltpu.VMEM((ts + CONV_HALO + CONV_TAIL, d), F32), pltpu.VMEM((ts, d), F32)],
        compiler_params=_params(2),
        name="conv_seq",
    )(x, gains, wi, bi, dw, dwb, lng, lnb, w, b)


def _conv_step_kernel(st_ref, u_ref, x_ref, dw_ref, dwb_ref, lng_ref, lnb_ref, w_ref, b_ref,
                      o_ref, ns_ref):
    hist = CONV_WIDTH - 1
    u = u_ref[...]
    c = u * dw_ref[hist:hist + 1, :] + dwb_ref[...]
    for j in range(hist):
        c = c + st_ref[j] * dw_ref[j:j + 1, :]
    ns_ref[0:hist - 1] = st_ref[1:hist]
    ns_ref[hist - 1] = u
    o_ref[...] = _conv_tail(c, x_ref[...], lng_ref, lnb_ref, w_ref, b_ref)


def _conv_step(state, u, x, dw, dwb, lng, lnb, w, b, l):
    n, d = u.shape
    nb = min(16, n)
    assert n % nb == 0
    hist = CONV_WIDTH - 1
    tok = pl.BlockSpec((nb, d), lambda r: (r, 0))
    vec = _resident((None, 1, d), lambda r: (l, 0, 0))
    return pl.pallas_call(
        _conv_step_kernel,
        grid=(n // nb,),
        in_specs=[
            pl.BlockSpec((None, hist, nb, d), lambda r: (l, 0, r, 0)),
            tok, tok,
            _resident((None, CONV_WIDTH, d), lambda r: (l, 0, 0)),
            vec, vec, vec,
            _resident((None, d, d), lambda r: (l, 0, 0)),
            vec,
        ],
        out_specs=[tok, pl.BlockSpec((hist, nb, d), lambda r: (0, r, 0))],
        out_shape=[jax.ShapeDtypeStruct((n, d), F32), jax.ShapeDtypeStruct((hist, n, d), F32)],
        compiler_params=_params(1),
        name="conv_step",
    )(state, u, x, dw, dwb, lng, lnb, w, b)


def _lambda(lam_ref, lam_init):
    lv = lam_ref[...]
    a = jnp.sum(lv[0:1] * lv[1:2], axis=-1, keepdims=True)
    b = jnp.sum(lv[2:3] * lv[3:4], axis=-1, keepdims=True)
    return jnp.exp(a) - jnp.exp(b) + lam_init


def _sub_norm(o, g, lam_init):
    return o * lax.rsqrt(jnp.mean(o * o, axis=-1, keepdims=True) + EPS) * g * (1.0 - lam_init)


def _flash_kernel(lam_ref, g_ref, q_ref, k_ref, vt_ref, o_ref, qq_ref, acc_ref, *, lam_init):
    tq = tk = vt_ref.shape[2]
    lam = _lambda(lam_ref, lam_init)

    def steps(js, carry, mask_last):
        half = tq // 2
        pieces = []
        for n, j in enumerate(js):
            if mask_last and n == len(js) - 1:
                pieces += [(j, slice(0, half), slice(0, half), True),
                           (j, slice(half, tq), slice(0, tk), True)]
            else:
                pieces.append((j, slice(0, tq), slice(0, tk), False))
        ss = []
        for j, cols, keys, _ in pieces:
            kc = k_ref[0, pl.ds(pl.multiple_of(j * tk, tk) + keys.start, keys.stop - keys.start), :]
            ss.append([lax.dot_general(kc, qq_ref[c, cols, :], (((1,), (1,)), ((), ())),
                                       preferred_element_type=F32) for c in range(2)])
        carry = [list(st) for st in carry]
        work = []
        for (j, cols, keys, masked), s_pair in zip(pieces, ss):
            for c in range(2):
                m_old, l_old = carry[c][0][:, cols], carry[c][1][:, cols]
                s = s_pair[c]
                if masked:
                    kpos = keys.start + lax.broadcasted_iota(jnp.int32, s.shape, 0)
                    qpos = cols.start + lax.broadcasted_iota(jnp.int32, s.shape, 1)
                    s = jnp.where(kpos <= qpos, s, NEG_BIG)
                m_new = jnp.maximum(m_old, jnp.max(s, axis=0, keepdims=True))
                alpha = jnp.exp2(m_old - m_new)
                p = jnp.exp2(s - m_new)
                l_new = alpha * l_old + jnp.sum(p, axis=0, keepdims=True)
                for idx, new in ((0, m_new), (1, l_new)):
                    old = carry[c][idx]
                    parts = ([old[:, :cols.start]] if cols.start else []) + [new]
                    parts += [old[:, cols.stop:]] if cols.stop < tq else []
                    carry[c][idx] = jnp.concatenate(parts, axis=1)
                work.append((c, j, cols, keys, alpha, p.astype(BF16)))
        for c, j, cols, keys, alpha, p in work:
            acc_ref[c, :, cols] = alpha * acc_ref[c, :, cols] + jnp.dot(
                vt_ref[j, :, keys], p, preferred_element_type=F32)
        return tuple(tuple(st) for st in carry)

    def query_tile(i, _):
        rows = pl.ds(pl.multiple_of(i * tq, tq), tq)
        q = q_ref[0, rows, :]
        lane = lax.broadcasted_iota(jnp.int32, q.shape, 1)
        zero = jnp.zeros_like(q)
        qq_ref[0] = jnp.where(lane < HEAD_DIM, q, zero)
        qq_ref[1] = jnp.where(lane >= HEAD_DIM, q, zero)
        acc_ref[...] = jnp.zeros(acc_ref.shape, F32)
        stat = (jnp.full((1, tq), NEG_BIG, F32), jnp.zeros((1, tq), F32))
        carry = lax.fori_loop(0, i // 2, lambda jj, c: steps([2 * jj, 2 * jj + 1], c, False),
                              (stat, stat))
        (_, l1), (_, l2) = lax.cond(i % 2 == 1,
                                    lambda c: steps([i - 1, i], c, True),
                                    lambda c: steps([i], c, True), carry)
        ot = acc_ref[0] * (1.0 / l1) - lam * (acc_ref[1] * (1.0 / l2))
        ot = ot * lax.rsqrt(jnp.mean(ot * ot, axis=0, keepdims=True) + EPS)
        ot = ot * (g_ref[...] * (1.0 - lam_init))
        o_ref[0, rows, :] = ot.T.astype(o_ref.dtype)
        return 0

    lax.fori_loop(0, q_ref.shape[1] // tq, query_tile, 0)


def _flash_attn(q, k, vt, lam, g_col, j, lam_init):
    bsz, s, d = q.shape
    tile = vt.shape[4]
    assert s % tile == 0 and vt.shape[2] == s // tile
    head = pl.BlockSpec((1, s, HEAD_W), lambda bi, h: (bi, 0, h))
    return pl.pallas_call(
        functools.partial(_flash_kernel, lam_init=lam_init),
        grid=(bsz, d // HEAD_W),
        in_specs=[
            pl.BlockSpec((None, 4, HEAD_DIM), lambda bi, h: (j, 0, 0)),
            pl.BlockSpec((None, HEAD_W, 1), lambda bi, h: (j, 0, 0)),
            head, head,
            pl.BlockSpec((None, None, s // tile, HEAD_W, tile), lambda bi, h: (bi, h, 0, 0, 0)),
        ],
        out_specs=head,
        out_shape=jax.ShapeDtypeStruct((bsz, s, d), BF16),
        scratch_shapes=[pltpu.VMEM((2, tile, HEAD_W), BF16), pltpu.VMEM((2, HEAD_W, tile), F32)],
        compiler_params=_params(2),
        name="flash_diff_attn",
    )(lam, g_col, q, k, vt)


def _decode_kernel(lam_ref, g_ref, q_ref, kn_ref, vn_ref, k_ref, v_ref, o_ref, *, lam_init):
    n_heads = q_ref.shape[1]
    shape = (2 * n_heads, HEAD_W)
    own_half = (lax.broadcasted_iota(jnp.int32, shape, 1) // HEAD_DIM ==
                lax.broadcasted_iota(jnp.int32, shape, 0) // n_heads)
    lam = _lambda(lam_ref, lam_init)
    for sq in range(q_ref.shape[0]):
        q = q_ref[sq].astype(F32)
        qrows = jnp.where(own_half, jnp.concatenate([q, q], axis=0), 0.0)
        s = lax.dot_general(qrows.astype(BF16), k_ref[sq], (((1,), (1,)), ((), ())),
                            preferred_element_type=F32)
        own = (lax.broadcasted_iota(jnp.int32, s.shape, 1) % n_heads ==
               lax.broadcasted_iota(jnp.int32, s.shape, 0) % n_heads)
        s = jnp.where(own, s, NEG_BIG)
        kn = kn_ref[sq]
        vn = vn_ref[sq]
        s_new = jnp.sum(qrows * jnp.concatenate([kn, kn], axis=0), axis=1, keepdims=True)
        m = jnp.maximum(jnp.max(s, axis=1, keepdims=True), s_new)
        p = jnp.exp2(s - m)
        p_new = jnp.exp2(s_new - m)
        inv_l = 1.0 / (jnp.sum(p, axis=1, keepdims=True) + p_new)
        o = jnp.dot(p.astype(BF16), v_ref[sq], preferred_element_type=F32)
        o = (o + p_new * jnp.concatenate([vn, vn], axis=0)) * inv_l
        res = o[:n_heads] - lam * o[n_heads:]
        o_ref[sq] = _sub_norm(res, g_ref[...], lam_init).astype(o_ref.dtype)


def _decode_attn(q, k_new, v_new, k_slab, v_slab, first_seq, lam, g, j, lam_init):
    n, rows, _ = k_slab.shape
    n_heads = q.shape[1]
    per = DECODE_SEQS if n % DECODE_SEQS == 0 else 1
    assert first_seq % per == 0
    row = pl.BlockSpec((per, n_heads, HEAD_W), lambda b: (first_seq // per + b, 0, 0))
    slab = pl.BlockSpec((per, rows, HEAD_W), lambda b: (b, 0, 0))
    return pl.pallas_call(
        functools.partial(_decode_kernel, lam_init=lam_init),
        grid=(n // per,),
        in_specs=[
            pl.BlockSpec((None, 4, HEAD_DIM), lambda b: (j, 0, 0)),
            pl.BlockSpec((None, 1, HEAD_W), lambda b: (j, 0, 0)),
            row, row, row, slab, slab,
        ],
        out_specs=pl.BlockSpec((per, n_heads, HEAD_W), lambda b: (b, 0, 0)),
        out_shape=jax.ShapeDtypeStruct((n, n_heads, HEAD_W), BF16),
        compiler_params=_params(1),
        name="decode_attn",
    )(lam, g, q, k_new, v_new, k_slab, v_slab)


def _rope_tables(pos):
    half = HEAD_DIM // 2
    freqs = ROPE_THETA ** (-jnp.arange(half, dtype=F32) * 2.0 / HEAD_DIM)
    ang = pos.astype(F32)[:, None] * freqs[None, :]
    cos, sin = jnp.cos(ang), jnp.sin(ang)
    return jnp.tile(cos, (1, 4)), jnp.tile(jnp.concatenate([-sin, sin], axis=1), (1, 2))


def kernel(x_prompt, x_sample, state_conv, cache_k, cache_v, page_table, norm_ffn, ffn_w_in, ffn_w_out, norm_mix, conv_w_in, conv_b_in, conv_dw, conv_dw_b, conv_ln_g, conv_ln_b, conv_w_out, conv_b_out, kv_norm, w_kv, w_q, lambda_q1, lambda_k1, lambda_q2, lambda_k2, subln_g, w_o, norm_final):
    bsz, s, d = x_prompt.shape
    n_dec, dec_seq, _ = x_sample.shape
    assert dec_seq == 1
    depth = norm_mix.shape[0]
    n_a = conv_w_in.shape[0]
    n_heads = d // HEAD_W
    past_len = page_table.shape[1] * cache_k.shape[1]

    w_in = ffn_w_in.astype(BF16)
    w_out = ffn_w_out.astype(BF16)
    cw_in = conv_w_in.astype(BF16)
    cw_out = conv_w_out.astype(BF16)
    wkv = w_kv.astype(BF16)
    wq = w_q.astype(BF16)
    wo = w_o.astype(BF16)
    g_ffn = norm_ffn[:, :, None, :]
    g_mix = norm_mix[:, None, :]
    g_kv = kv_norm[None, :]
    g_fin = norm_final[None, :]
    cb_in = conv_b_in[:, None, :]
    cdw_b = conv_dw_b[:, None, :]
    cln_g = conv_ln_g[:, None, :]
    cln_b = conv_ln_b[:, None, :]
    cb_out = conv_b_out[:, None, :]
    lam = jnp.stack([lambda_q1, lambda_k1, lambda_q2, lambda_k2], axis=1)
    g_sub = subln_g[:, None, :]
    g_sub_col = subln_g[:, :, None]

    q_scale = HEAD_DIM ** -0.5 * math.log2(math.e)

    def run(x, cos, sin, pos_tiles, ffn_fn, conv_fn, attn_fn, attn_layout):
        conv_rows = []
        k = v = kv_ctx = None
        for l in range(depth):
            if l == n_a:
                kv_ctx = _kv_proj(x, g_kv, wkv, cos, sin, pos_tiles, attn_layout)
                k, v = kv_ctx[:2]
            if l < n_a:
                x = ffn_fn(x, l, 0)[0]
                x, state = conv_fn(x, l)
                conv_rows.append(state)
                x = ffn_fn(x, l, 1)[0]
            else:
                j = l - n_a
                lam_init = 0.8 - 0.6 * math.exp(-0.3 * l)
                x, q = ffn_fn(x, l, 0, query=(g_mix, wq, j, cos, sin, q_scale))[:2]
                o = attn_fn(q, kv_ctx, j, lam_init)
                x = ffn_fn(x, l, 1, attn=(o, wo, j))[0]
        return x, conv_rows, k, v

    def final_gain(l, i):
        return g_fin if (l == depth - 1 and i == 1) else None

    cos_p, sin_p = _rope_tables(jnp.arange(s, dtype=jnp.int32))
    tiles_p = s // min(TOKEN_TILE, s)

    n_groups = depth
    group = n_dec // n_groups
    assert n_dec % n_groups == 0
    pool, page = cache_k.shape[:2]
    caches = (cache_k.reshape(pool, page * n_heads, HEAD_W),
              cache_v.reshape(pool, page * n_heads, HEAD_W))
    slabs = ([], [])

    def ffn_prompt(x, l, i, **fused):
        call = 2 * l + i
        which, grp = divmod(call, n_groups)
        *outs, slab = _ffn(x, g_ffn, w_in, w_out, l, i, final_gain(l, i),
                           pages=(page_table, caches[which], grp * group, group), **fused)
        slabs[which].append(slab)
        return outs

    def conv_prompt(x, l):
        y, state = _conv_seq(x.reshape(bsz, s, d), g_mix, cw_in, cb_in, conv_dw, cdw_b, cln_g,
                             cln_b, cw_out, cb_out, l)
        return y.reshape(bsz * s, d), state

    def attn_prompt(q, kv_ctx, j, lam_init):
        _, _, kb, vt = kv_ctx
        o = _flash_attn(q.reshape(bsz, s, d), kb.reshape(bsz, s, d), vt, lam, g_sub_col, j,
                        lam_init)
        return o.reshape(bsz * s, d)

    y_p, conv_p, k_p, v_p = run(x_prompt.reshape(bsz * s, d), cos_p, sin_p, tiles_p,
                                ffn_prompt, conv_prompt, attn_prompt, True)
    conv_p = jnp.stack(conv_p, axis=0)

    cos_s, sin_s = _rope_tables(jnp.full((n_dec,), past_len, dtype=jnp.int32))

    state_hist = state_conv.transpose(0, 2, 1, 3)

    def conv_sample(x, l):
        u = _glu_proj(x, g_mix, cw_in, cb_in, l)
        return _conv_step(state_hist, u, x, conv_dw, cdw_b, cln_g, cln_b, cw_out, cb_out, l)

    def ffn_sample(x, l, i, **fused):
        return _ffn(x, g_ffn, w_in, w_out, l, i, final_gain(l, i), **fused)

    def attn_sample(q, kv_ctx, j, lam_init):
        k, v = kv_ctx
        per_head = (n_dec, n_heads, HEAD_W)
        rows = [_decode_attn(q.reshape(per_head), k.reshape(per_head), v.reshape(per_head),
                             slabs[0][grp], slabs[1][grp], grp * group, lam, g_sub, j, lam_init)
                for grp in range(n_groups)]
        return jnp.concatenate(rows, axis=0).reshape(n_dec, d)

    y_s, conv_s, k_s, v_s = run(x_sample.reshape(n_dec, d), cos_s, sin_s, 1,
                                ffn_sample, conv_sample, attn_sample, False)
    conv_s = jnp.stack(conv_s, axis=0).transpose(0, 2, 1, 3)

    return (y_p.reshape(bsz, s, d), y_s.reshape(n_dec, 1, d), conv_p,
            k_p.reshape(bsz, s, n_heads, HEAD_W), v_p.reshape(bsz, s, n_heads, HEAD_W),
            conv_s, k_s.reshape(n_dec, 1, n_heads, HEAD_W), v_s.reshape(n_dec, 1, n_heads, HEAD_W))
```

```python
import functools
import math

import jax
import jax.numpy as jnp
from jax import lax
from jax.experimental import pallas as pl
from jax.experimental.pallas import tpu as pltpu

HEAD_DIM = 64
HEAD_W = 2 * HEAD_DIM
CONV_WIDTH = 31
CONV_HALO = 32
ROPE_THETA = 10000.0
EPS = 1e-6
NEG_BIG = -1e30

TOKEN_TILE = 512
PROJ_TILE = 1024
FFN_CHUNK = 256
DECODE_SEQS = 2
CONV_ROWS = 8
CONV_TAIL = CONV_ROWS
VMEM_LIMIT = 60 * 1024 * 1024

F32 = jnp.float32
BF16 = jnp.bfloat16


def _params(n_axes, vmem=VMEM_LIMIT):
    return pltpu.CompilerParams(dimension_semantics=("arbitrary",) * n_axes,
                                vmem_limit_bytes=vmem)


def _rms(x, g):
    return x * lax.rsqrt(jnp.mean(x * x, axis=-1, keepdims=True) + EPS) * g


def _resident(shape, index_map):
    return pl.BlockSpec(shape, index_map, pipeline_mode=pl.Buffered(1))


def _compact_pages(page_refs, out_ref, n_pages):
    for idx, ref in enumerate(page_refs):
        sidx, p = divmod(idx, n_pages)
        rows = ref.shape[1]
        out_ref[sidx, p * rows:(p + 1) * rows, :] = ref[0].astype(BF16)


def _ffn_kernel(*refs, final_norm, attn_in, q_scale, n_page_refs, n_pages):
    refs = list(refs)
    if n_page_refs:
        refs.pop(0)
    x_ref, g_ref, wg_ref, wu_ref, wo_ref = refs[:5]
    del refs[:5]
    if final_norm:
        gf_ref = refs.pop(0)
    if attn_in:
        a_ref, wa_ref = refs[:2]
        del refs[:2]
    if q_scale is not None:
        gq_ref, wq_ref, cos_ref, sin_ref = refs[:4]
        del refs[:4]
    page_refs = refs[:n_page_refs]
    del refs[:n_page_refs]
    o_ref = refs.pop(0)
    if q_scale is not None:
        q_ref = refs.pop(0)
    if n_page_refs:
        _compact_pages(page_refs, refs.pop(0), n_pages)
    xn_ref, h_ref = refs
    x = x_ref[...]
    if attn_in:
        x = x + jnp.dot(a_ref[...], wa_ref[...], preferred_element_type=F32)
    xn_ref[...] = _rms(x, g_ref[...]).astype(BF16)
    d_ff = h_ref.shape[1]
    for c in range(0, d_ff, FFN_CHUNK):
        xn = xn_ref[...]
        gate = jnp.dot(xn, wg_ref[:, c:c + FFN_CHUNK], preferred_element_type=F32)
        up = jnp.dot(xn, wu_ref[:, c:c + FFN_CHUNK], preferred_element_type=F32)
        h_ref[:, c:c + FFN_CHUNK] = (gate * jax.nn.sigmoid(gate) * up).astype(BF16)
    y = x + 0.5 * jnp.dot(h_ref[...], wo_ref[...], preferred_element_type=F32)
    if final_norm:
        y = _rms(y, gf_ref[...])
    o_ref[...] = y
    if q_scale is not None:
        yq = jnp.dot(_rms(y, gq_ref[...]).astype(BF16), wq_ref[...], preferred_element_type=F32)
        cos = cos_ref[...]
        sin = sin_ref[...]
        for h in range(yq.shape[1] // HEAD_W):
            sl = slice(h * HEAD_W, (h + 1) * HEAD_W)
            q_ref[:, sl] = (_rope_head(yq[:, sl], cos, sin) * q_scale).astype(q_ref.dtype)


def _ffn(x, gains, w_in, w_out, l, i, final_gain=None, pages=None, attn=None, query=None):
    t, d = x.shape
    f = w_out.shape[2]
    assert f % FFN_CHUNK == 0
    tm = min(TOKEN_TILE, t)
    assert t % tm == 0
    steps = t // tm
    tok = pl.BlockSpec((tm, d), lambda r, *_: (r, 0))
    in_specs = [
        tok,
        _resident((None, None, 1, d), lambda r, *_: (l, i, 0, 0)),
        _resident((None, None, d, f), lambda r, *_: (l, i, 0, 0)),
        _resident((None, None, d, f), lambda r, *_: (l, i, 0, 1)),
        _resident((None, None, f, d), lambda r, *_: (l, i, 0, 0)),
    ]
    args = [x, gains, w_in, w_in, w_out]
    out_specs = [tok]
    out_shape = [jax.ShapeDtypeStruct((t, d), F32)]
    if final_gain is not None:
        in_specs.append(_resident((1, d), lambda r, *_: (0, 0)))
        args.append(final_gain)
    if attn is not None:
        a, wa, ja = attn
        in_specs += [tok, _resident((None, d, d), lambda r, *_: (ja, 0, 0))]
        args += [a, wa]
    q_scale = None
    if query is not None:
        gq, wq, jq, cos, sin, q_scale = query
        pos_tiles = cos.shape[0] // tm
        pos = pl.BlockSpec((tm, HEAD_W), lambda r, *_: (r % pos_tiles, 0))
        in_specs += [_resident((None, 1, d), lambda r, *_: (l, 0, 0)),
                     _resident((None, d, d), lambda r, *_: (jq, 0, 0)), pos, pos]
        args += [gq, wq, cos, sin]
        out_specs.append(tok)
        out_shape.append(jax.ShapeDtypeStruct((t, d), BF16))
    n_page_refs = n_pages = 0
    if pages is not None:
        page_table, cache, first_seq, n_seq = pages
        n_pages = page_table.shape[1]
        rows = cache.shape[1]
        assert n_seq % steps == 0
        per_step = n_seq // steps
        for sq in range(per_step):
            for p in range(n_pages):
                in_specs.append(pl.BlockSpec(
                    (1, rows, HEAD_W),
                    lambda r, pt, sq=sq, p=p: (pt[first_seq + r * per_step + sq, p], 0, 0)))
                args.append(cache)
        n_page_refs = per_step * n_pages
        out_specs.append(pl.BlockSpec((per_step, n_pages * rows, HEAD_W),
                                      lambda r, pt: (r, 0, 0)))
        out_shape.append(jax.ShapeDtypeStruct((n_seq, n_pages * rows, HEAD_W), BF16))
        args = [page_table] + args
    grid_spec = pltpu.PrefetchScalarGridSpec(
        num_scalar_prefetch=0 if pages is None else 1,
        grid=(steps,),
        in_specs=in_specs,
        out_specs=out_specs,
        scratch_shapes=[pltpu.VMEM((tm, d), BF16), pltpu.VMEM((tm, f), BF16)],
    )
    return pl.pallas_call(
        functools.partial(_ffn_kernel, final_norm=final_gain is not None,
                          attn_in=attn is not None, q_scale=q_scale,
                          n_page_refs=n_page_refs, n_pages=n_pages),
        grid_spec=grid_spec,
        out_shape=out_shape,
        compiler_params=_params(1),
        name="ffn",
    )(*args)


def _glu_kernel(x_ref, g_ref, w_ref, b_ref, u_ref):
    d = u_ref.shape[1]
    xn = _rms(x_ref[...], g_ref[...]).astype(BF16)
    y = jnp.dot(xn, w_ref[...], preferred_element_type=F32) + b_ref[...]
    u_ref[...] = y[:, :d] * jax.nn.sigmoid(y[:, d:])


def _glu_proj(x, gains, w, b, l):
    t, d = x.shape
    tm = min(PROJ_TILE, t)
    return pl.pallas_call(
        _glu_kernel,
        grid=(t // tm,),
        in_specs=[
            pl.BlockSpec((tm, d), lambda r: (r, 0)),
            _resident((None, 1, d), lambda r: (l, 0, 0)),
            _resident((None, d, 2 * d), lambda r: (l, 0, 0)),
            _resident((None, 1, 2 * d), lambda r: (l, 0, 0)),
        ],
        out_specs=pl.BlockSpec((tm, d), lambda r: (r, 0)),
        out_shape=jax.ShapeDtypeStruct((t, d), F32),
        compiler_params=_params(1),
        name="conv_glu",
    )(x, gains, w, b)


def _rope_head(xh, cos, sin_signed):
    half = HEAD_DIM // 2
    lane = lax.broadcasted_iota(jnp.int32, xh.shape, 1)
    partner = jnp.where(lane % HEAD_DIM < half,
                        pltpu.roll(xh, HEAD_W - half, 1),
                        pltpu.roll(xh, half, 1))
    return xh * cos + partner * sin_signed


def _kv_kernel(x_ref, g_ref, w_ref, cos_ref, sin_ref, k_ref, v_ref, *attn_refs):
    d = k_ref.shape[1]
    xn = _rms(x_ref[...], g_ref[...]).astype(BF16)
    y = jnp.dot(xn, w_ref[...], preferred_element_type=F32)
    v_ref[...] = y[:, d:]
    cos = cos_ref[...]
    sin = sin_ref[...]
    for h in range(d // HEAD_W):
        sl = slice(h * HEAD_W, (h + 1) * HEAD_W)
        kh = _rope_head(y[:, sl], cos, sin)
        k_ref[:, sl] = kh
        if attn_refs:
            kb_ref, vt_ref = attn_refs
            kb_ref[:, sl] = kh.astype(BF16)
            vt_ref[h] = y[:, d + h * HEAD_W:d + (h + 1) * HEAD_W].T.astype(BF16)


def _kv_proj(x, gain, w, cos, sin, pos_tiles, attn_layout):
    t, d = x.shape
    tm = min(TOKEN_TILE, t)
    tok = pl.BlockSpec((tm, d), lambda r: (r, 0))
    pos = pl.BlockSpec((tm, HEAD_W), lambda r: (r % pos_tiles, 0))
    out_specs = [tok, tok]
    out_shape = [jax.ShapeDtypeStruct((t, d), F32), jax.ShapeDtypeStruct((t, d), F32)]
    if attn_layout:
        n_heads = d // HEAD_W
        out_specs += [tok, pl.BlockSpec((None, n_heads, None, HEAD_W, tm),
                                        lambda r: (r // pos_tiles, 0, r % pos_tiles, 0, 0))]
        out_shape += [jax.ShapeDtypeStruct((t, d), BF16),
                      jax.ShapeDtypeStruct((t // (pos_tiles * tm), n_heads, pos_tiles, HEAD_W, tm),
                                           BF16)]
    return pl.pallas_call(
        _kv_kernel,
        grid=(t // tm,),
        in_specs=[tok, _resident((1, d), lambda r: (0, 0)),
                  _resident((d, 2 * d), lambda r: (0, 0)), pos, pos],
        out_specs=out_specs,
        out_shape=out_shape,
        compiler_params=_params(1),
        name="kv_proj",
    )(x, gain, w, cos, sin)


def _conv_tail(c, x, lng_ref, lnb_ref, w_ref, b_ref):
    mu = jnp.mean(c, axis=-1, keepdims=True)
    xc = c - mu
    y = xc * lax.rsqrt(jnp.mean(xc * xc, axis=-1, keepdims=True) + EPS)
    y = y * lng_ref[...] + lnb_ref[...]
    z = (y * jax.nn.sigmoid(y)).astype(BF16)
    return x + jnp.dot(z, w_ref[...], preferred_element_type=F32) + b_ref[...]


def _conv_seq_kernel(x_ref, g_ref, wi_ref, bi_ref, dw_ref, dwb_ref, lng_ref, lnb_ref, w_ref, b_ref,
                     o_ref, st_ref, full_ref, c_ref):
    ts, d = c_ref.shape
    first = pl.program_id(1) == 0

    @pl.when(first)
    def _():
        full_ref[0:CONV_HALO, :] = jnp.zeros((CONV_HALO, d), F32)

    @pl.when(jnp.logical_not(first))
    def _():
        full_ref[0:CONV_HALO, :] = full_ref[ts:ts + CONV_HALO, :]

    xn = _rms(x_ref[0], g_ref[...]).astype(BF16)
    y = jnp.dot(xn, wi_ref[...], preferred_element_type=F32) + bi_ref[...]
    full_ref[CONV_HALO:CONV_HALO + ts, :] = y[:, :d] * jax.nn.sigmoid(y[:, d:])
    full_ref[CONV_HALO + ts:, :] = jnp.zeros((CONV_TAIL, d), F32)

    @pl.when(pl.program_id(1) == pl.num_programs(1) - 1)
    def _():
        hist = st_ref.shape[1]
        st_ref[0] = full_ref[CONV_HALO + ts - hist:CONV_HALO + ts, :]
    lead = CONV_HALO - (CONV_WIDTH - 1)

    n_a = (lead + CONV_WIDTH - 1) // CONV_ROWS + 1
    sub = lax.broadcasted_iota(jnp.int32, (CONV_ROWS, HEAD_W), 0)

    for cb in range(d // HEAD_W):
        sl = slice(cb * HEAD_W, (cb + 1) * HEAD_W)
        taps = {}
        for a in range(n_a):
            for b in range(CONV_ROWS):
                j = CONV_ROWS * a + b - lead
                if 0 <= j < CONV_WIDTH:
                    taps[a, b] = jnp.broadcast_to(dw_ref[j:j + 1, sl], (CONV_ROWS, HEAD_W))
        bias = jnp.broadcast_to(dwb_ref[:, sl], (CONV_ROWS, HEAD_W))

        def partials(r):
            base = r * CONV_ROWS
            rows = [full_ref[pl.ds(base + CONV_ROWS * a, CONV_ROWS), sl] for a in range(n_a)]
            out = []
            for b in range(CONV_ROWS):
                z = None
                for a in range(n_a):
                    if (a, b) in taps:
                        term = rows[a] * taps[a, b]
                        z = term if z is None else z + term
                out.append(z)
            return tuple(out)

        def block(r, z_prev):
            z_next = partials(r)
            acc = bias + z_prev[0]
            for b in range(1, CONV_ROWS):
                mixed = jnp.where(sub >= b, z_prev[b], z_next[b])
                acc = acc + pltpu.roll(mixed, CONV_ROWS - b, 0)
            c_ref[pl.ds((r - 1) * CONV_ROWS, CONV_ROWS), sl] = acc
            return z_next

        z = partials(0)
        for r in range(1, ts // CONV_ROWS + 1):
            z = block(r, z)

    o_ref[0] = _conv_tail(c_ref[...], x_ref[0], lng_ref, lnb_ref, w_ref, b_ref)


def _conv_seq(x, gains, wi, bi, dw, dwb, lng, lnb, w, b, l):
    bsz, s, d = x.shape
    hist = CONV_WIDTH - 1
    ts = min(TOKEN_TILE, s)
    assert s % ts == 0 and ts >= CONV_HALO
    tile = pl.BlockSpec((1, ts, d), lambda bi, i: (bi, i, 0))
    vec = _resident((None, 1, d), lambda bi, i: (l, 0, 0))
    return pl.pallas_call(
        _conv_seq_kernel,
        grid=(bsz, s // ts),
        in_specs=[
            tile,
            vec,
            _resident((None, d, 2 * d), lambda bi, i: (l, 0, 0)),
            _resident((None, 1, 2 * d), lambda bi, i: (l, 0, 0)),
            _resident((None, CONV_WIDTH, d), lambda bi, i: (l, 0, 0)),
            vec, vec, vec,
            _resident((None, d, d), lambda bi, i: (l, 0, 0)),
            vec,
        ],
        out_specs=[tile, pl.BlockSpec((1, hist, d), lambda bi, i: (bi, 0, 0))],
        out_shape=[jax.ShapeDtypeStruct((bsz, s, d), F32),
                   jax.ShapeDtypeStruct((bsz, hist, d), F32)],
        scratch_shapes=[pltpu.VMEM((ts + CONV_HALO + CONV_TAIL, d), F32), pltpu.VMEM((ts, d), F32)],
        compiler_params=_params(2),
        name="conv_seq",
    )(x, gains, wi, bi, dw, dwb, lng, lnb, w, b)


def _conv_step_kernel(st_ref, u_ref, x_ref, dw_ref, dwb_ref, lng_ref, lnb_ref, w_ref, b_ref,
                      o_ref, ns_ref):
    hist = CONV_WIDTH - 1
    u = u_ref[...]
    c = u * dw_ref[hist:hist + 1, :] + dwb_ref[...]
    for j in range(hist):
        c = c + st_ref[j] * dw_ref[j:j + 1, :]
    ns_ref[0:hist - 1] = st_ref[1:hist]
    ns_ref[hist - 1] = u
    o_ref[...] = _conv_tail(c, x_ref[...], lng_ref, lnb_ref, w_ref, b_ref)


def _conv_step(state, u, x, dw, dwb, lng, lnb, w, b, l):
    n, d = u.shape
    nb = min(16, n)
    assert n % nb == 0
    hist = CONV_WIDTH - 1
    tok = pl.BlockSpec((nb, d), lambda r: (r, 0))
    vec = _resident((None, 1, d), lambda r: (l, 0, 0))
    return pl.pallas_call(
        _conv_step_kernel,
        grid=(n // nb,),
        in_specs=[
            pl.BlockSpec((None, hist, nb, d), lambda r: (l, 0, r, 0)),
            tok, tok,
            _resident((None, CONV_WIDTH, d), lambda r: (l, 0, 0)),
            vec, vec, vec,
            _resident((None, d, d), lambda r: (l, 0, 0)),
            vec,
        ],
        out_specs=[tok, pl.BlockSpec((hist, nb, d), lambda r: (0, r, 0))],
        out_shape=[jax.ShapeDtypeStruct((n, d), F32), jax.ShapeDtypeStruct((hist, n, d), F32)],
        compiler_params=_params(1),
        name="conv_step",
    )(state, u, x, dw, dwb, lng, lnb, w, b)


def _lambda(lam_ref, lam_init):
    lv = lam_ref[...]
    a = jnp.sum(lv[0:1] * lv[1:2], axis=-1, keepdims=True)
    b = jnp.sum(lv[2:3] * lv[3:4], axis=-1, keepdims=True)
    return jnp.exp(a) - jnp.exp(b) + lam_init


def _sub_norm(o, g, lam_init):
    return o * lax.rsqrt(jnp.mean(o * o, axis=-1, keepdims=True) + EPS) * g * (1.0 - lam_init)


def _flash_kernel(lam_ref, g_ref, q_ref, k_ref, vt_ref, o_ref, qq_ref, acc_ref, *, lam_init):
    tq = tk = vt_ref.shape[2]
    lam = _lambda(lam_ref, lam_init)

    def steps(js, carry, mask_last):
        half = tq // 2
        pieces = []
        for n, j in enumerate(js):
            if mask_last and n == len(js) - 1:
                pieces += [(j, slice(0, half), slice(0, half), True),
                           (j, slice(half, tq), slice(0, tk), True)]
            else:
                pieces.append((j, slice(0, tq), slice(0, tk), False))
        ss = []
        for j, cols, keys, _ in pieces:
            kc = k_ref[0, pl.ds(j * tk + keys.start, keys.stop - keys.start), :]
            ss.append([lax.dot_general(kc, qq_ref[c, cols, :], (((1,), (1,)), ((), ())),
                                       preferred_element_type=F32) for c in range(2)])
        carry = [list(st) for st in carry]
        work = []
        for (j, cols, keys, masked), s_pair in zip(pieces, ss):
            for c in range(2):
                m_old, l_old = carry[c][0][:, cols], carry[c][1][:, cols]
                s = s_pair[c]
                if masked:
                    kpos = keys.start + lax.broadcasted_iota(jnp.int32, s.shape, 0)
                    qpos = cols.start + lax.broadcasted_iota(jnp.int32, s.shape, 1)
                    s = jnp.where(kpos <= qpos, s, NEG_BIG)
                m_new = jnp.maximum(m_old, jnp.max(s, axis=0, keepdims=True))
                alpha = jnp.exp2(m_old - m_new)
                p = jnp.exp2(s - m_new)
                l_new = alpha * l_old + jnp.sum(p, axis=0, keepdims=True)
                for idx, new in ((0, m_new), (1, l_new)):
                    old = carry[c][idx]
                    parts = ([old[:, :cols.start]] if cols.start else []) + [new]
                    parts += [old[:, cols.stop:]] if cols.stop < tq else []
                    carry[c][idx] = jnp.concatenate(parts, axis=1)
                work.append((c, j, cols, keys, alpha, p.astype(BF16)))
        for c, j, cols, keys, alpha, p in work:
            acc_ref[c, :, cols] = alpha * acc_ref[c, :, cols] + jnp.dot(
                vt_ref[j, :, keys], p, preferred_element_type=F32)
        return tuple(tuple(st) for st in carry)

    def query_tile(i):
        rows = pl.ds(i * tq, tq)
        q = q_ref[0, rows, :]
        lane = lax.broadcasted_iota(jnp.int32, q.shape, 1)
        zero = jnp.zeros_like(q)
        qq_ref[0] = jnp.where(lane < HEAD_DIM, q, zero)
        qq_ref[1] = jnp.where(lane >= HEAD_DIM, q, zero)
        acc_ref[...] = jnp.zeros(acc_ref.shape, F32)
        stat = (jnp.full((1, tq), NEG_BIG, F32), jnp.zeros((1, tq), F32))
        carry = (stat, stat)
        for jj in range(i // 2):
            carry = steps([2 * jj, 2 * jj + 1], carry, False)
        (_, l1), (_, l2) = steps([i - 1, i] if i % 2 else [i], carry, True)
        ot = acc_ref[0] * (1.0 / l1) - lam * (acc_ref[1] * (1.0 / l2))
        ot = ot * lax.rsqrt(jnp.mean(ot * ot, axis=0, keepdims=True) + EPS)
        ot = ot * (g_ref[...] * (1.0 - lam_init))
        o_ref[0, rows, :] = ot.T.astype(o_ref.dtype)

    for i in range(q_ref.shape[1] // tq):
        query_tile(i)


def _flash_attn(q, k, vt, lam, g_col, j, lam_init):
    bsz, s, d = q.shape
    tile = vt.shape[4]
    assert s % tile == 0 and vt.shape[2] == s // tile
    head = pl.BlockSpec((1, s, HEAD_W), lambda bi, h: (bi, 0, h))
    return pl.pallas_call(
        functools.partial(_flash_kernel, lam_init=lam_init),
        grid=(bsz, d // HEAD_W),
        in_specs=[
            pl.BlockSpec((None, 4, HEAD_DIM), lambda bi, h: (j, 0, 0)),
            pl.BlockSpec((None, HEAD_W, 1), lambda bi, h: (j, 0, 0)),
            head, head,
            pl.BlockSpec((None, None, s // tile, HEAD_W, tile), lambda bi, h: (bi, h, 0, 0, 0)),
        ],
        out_specs=head,
        out_shape=jax.ShapeDtypeStruct((bsz, s, d), BF16),
        scratch_shapes=[pltpu.VMEM((2, tile, HEAD_W), BF16), pltpu.VMEM((2, HEAD_W, tile), F32)],
        compiler_params=_params(2),
        name="flash_diff_attn",
    )(lam, g_col, q, k, vt)


def _decode_kernel(lam_ref, g_ref, q_ref, kn_ref, vn_ref, k_ref, v_ref, o_ref, *, lam_init):
    n_heads = q_ref.shape[1]
    shape = (2 * n_heads, HEAD_W)
    own_half = (lax.broadcasted_iota(jnp.int32, shape, 1) // HEAD_DIM ==
                lax.broadcasted_iota(jnp.int32, shape, 0) // n_heads)
    lam = _lambda(lam_ref, lam_init)
    for sq in range(q_ref.shape[0]):
        q = q_ref[sq].astype(F32)
        qrows = jnp.where(own_half, jnp.concatenate([q, q], axis=0), 0.0)
        s = lax.dot_general(qrows.astype(BF16), k_ref[sq], (((1,), (1,)), ((), ())),
                            preferred_element_type=F32)
        own = (lax.broadcasted_iota(jnp.int32, s.shape, 1) % n_heads ==
               lax.broadcasted_iota(jnp.int32, s.shape, 0) % n_heads)
        s = jnp.where(own, s, NEG_BIG)
        kn = kn_ref[sq]
        vn = vn_ref[sq]
        s_new = jnp.sum(qrows * jnp.concatenate([kn, kn], axis=0), axis=1, keepdims=True)
        m = jnp.maximum(jnp.max(s, axis=1, keepdims=True), s_new)
        p = jnp.exp2(s - m)
        p_new = jnp.exp2(s_new - m)
        inv_l = 1.0 / (jnp.sum(p, axis=1, keepdims=True) + p_new)
        o = jnp.dot(p.astype(BF16), v_ref[sq], preferred_element_type=F32)
        o = (o + p_new * jnp.concatenate([vn, vn], axis=0)) * inv_l
        res = o[:n_heads] - lam * o[n_heads:]
        o_ref[sq] = _sub_norm(res, g_ref[...], lam_init).astype(o_ref.dtype)


def _decode_attn(q, k_new, v_new, k_slab, v_slab, first_seq, lam, g, j, lam_init):
    n, rows, _ = k_slab.shape
    n_heads = q.shape[1]
    per = DECODE_SEQS if n % DECODE_SEQS == 0 else 1
    assert first_seq % per == 0
    row = pl.BlockSpec((per, n_heads, HEAD_W), lambda b: (first_seq // per + b, 0, 0))
    slab = pl.BlockSpec((per, rows, HEAD_W), lambda b: (b, 0, 0))
    return pl.pallas_call(
        functools.partial(_decode_kernel, lam_init=lam_init),
        grid=(n // per,),
        in_specs=[
            pl.BlockSpec((None, 4, HEAD_DIM), lambda b: (j, 0, 0)),
            pl.BlockSpec((None, 1, HEAD_W), lambda b: (j, 0, 0)),
            row, row, row, slab, slab,
        ],
        out_specs=pl.BlockSpec((per, n_heads, HEAD_W), lambda b: (b, 0, 0)),
        out_shape=jax.ShapeDtypeStruct((n, n_heads, HEAD_W), BF16),
        compiler_params=_params(1),
        name="decode_attn",
    )(lam, g, q, k_new, v_new, k_slab, v_slab)


def _rope_tables(pos):
    half = HEAD_DIM // 2
    freqs = ROPE_THETA ** (-jnp.arange(half, dtype=F32) * 2.0 / HEAD_DIM)
    ang = pos.astype(F32)[:, None] * freqs[None, :]
    cos, sin = jnp.cos(ang), jnp.sin(ang)
    return jnp.tile(cos, (1, 4)), jnp.tile(jnp.concatenate([-sin, sin], axis=1), (1, 2))


def kernel(x_prompt, x_sample, state_conv, cache_k, cache_v, page_table, norm_ffn, ffn_w_in, ffn_w_out, norm_mix, conv_w_in, conv_b_in, conv_dw, conv_dw_b, conv_ln_g, conv_ln_b, conv_w_out, conv_b_out, kv_norm, w_kv, w_q, lambda_q1, lambda_k1, lambda_q2, lambda_k2, subln_g, w_o, norm_final):
    bsz, s, d = x_prompt.shape
    n_dec, dec_seq, _ = x_sample.shape
    assert dec_seq == 1
    depth = norm_mix.shape[0]
    n_a = conv_w_in.shape[0]
    n_heads = d // HEAD_W
    past_len = page_table.shape[1] * cache_k.shape[1]

    w_in = ffn_w_in.astype(BF16)
    w_out = ffn_w_out.astype(BF16)
    cw_in = conv_w_in.astype(BF16)
    cw_out = conv_w_out.astype(BF16)
    wkv = w_kv.astype(BF16)
    wq = w_q.astype(BF16)
    wo = w_o.astype(BF16)
    g_ffn = norm_ffn[:, :, None, :]
    g_mix = norm_mix[:, None, :]
    g_kv = kv_norm[None, :]
    g_fin = norm_final[None, :]
    cb_in = conv_b_in[:, None, :]
    cdw_b = conv_dw_b[:, None, :]
    cln_g = conv_ln_g[:, None, :]
    cln_b = conv_ln_b[:, None, :]
    cb_out = conv_b_out[:, None, :]
    lam = jnp.stack([lambda_q1, lambda_k1, lambda_q2, lambda_k2], axis=1)
    g_sub = subln_g[:, None, :]
    g_sub_col = subln_g[:, :, None]

    q_scale = HEAD_DIM ** -0.5 * math.log2(math.e)

    def run(x, cos, sin, pos_tiles, ffn_fn, conv_fn, attn_fn, attn_layout):
        conv_rows = []
        k = v = kv_ctx = None
        for l in range(depth):
            if l == n_a:
                kv_ctx = _kv_proj(x, g_kv, wkv, cos, sin, pos_tiles, attn_layout)
                k, v = kv_ctx[:2]
            if l < n_a:
                x = ffn_fn(x, l, 0)[0]
                x, state = conv_fn(x, l)
                conv_rows.append(state)
                x = ffn_fn(x, l, 1)[0]
            else:
                j = l - n_a
                lam_init = 0.8 - 0.6 * math.exp(-0.3 * l)
                x, q = ffn_fn(x, l, 0, query=(g_mix, wq, j, cos, sin, q_scale))[:2]
                o = attn_fn(q, kv_ctx, j, lam_init)
                x = ffn_fn(x, l, 1, attn=(o, wo, j))[0]
        return x, conv_rows, k, v

    def final_gain(l, i):
        return g_fin if (l == depth - 1 and i == 1) else None

    cos_p, sin_p = _rope_tables(jnp.arange(s, dtype=jnp.int32))
    tiles_p = s // min(TOKEN_TILE, s)

    n_groups = depth
    group = n_dec // n_groups
    assert n_dec % n_groups == 0
    pool, page = cache_k.shape[:2]
    caches = (cache_k.reshape(pool, page * n_heads, HEAD_W),
              cache_v.reshape(pool, page * n_heads, HEAD_W))
    slabs = ([], [])

    def ffn_prompt(x, l, i, **fused):
        call = 2 * l + i
        which, grp = divmod(call, n_groups)
        *outs, slab = _ffn(x, g_ffn, w_in, w_out, l, i, final_gain(l, i),
                           pages=(page_table, caches[which], grp * group, group), **fused)
        slabs[which].append(slab)
        return outs

    def conv_prompt(x, l):
        y, state = _conv_seq(x.reshape(bsz, s, d), g_mix, cw_in, cb_in, conv_dw, cdw_b, cln_g,
                             cln_b, cw_out, cb_out, l)
        return y.reshape(bsz * s, d), state

    def attn_prompt(q, kv_ctx, j, lam_init):
        _, _, kb, vt = kv_ctx
        o = _flash_attn(q.reshape(bsz, s, d), kb.reshape(bsz, s, d), vt, lam, g_sub_col, j,
                        lam_init)
        return o.reshape(bsz * s, d)

    y_p, conv_p, k_p, v_p = run(x_prompt.reshape(bsz * s, d), cos_p, sin_p, tiles_p,
                                ffn_prompt, conv_prompt, attn_prompt, True)
    conv_p = jnp.stack(conv_p, axis=0)

    cos_s, sin_s = _rope_tables(jnp.full((n_dec,), past_len, dtype=jnp.int32))

    state_hist = state_conv.transpose(0, 2, 1, 3)

    def conv_sample(x, l):
        u = _glu_proj(x, g_mix, cw_in, cb_in, l)
        return _conv_step(state_hist, u, x, conv_dw, cdw_b, cln_g, cln_b, cw_out, cb_out, l)

    def ffn_sample(x, l, i, **fused):
        return _ffn(x, g_ffn, w_in, w_out, l, i, final_gain(l, i), **fused)

    def attn_sample(q, kv_ctx, j, lam_init):
        k, v = kv_ctx
        per_head = (n_dec, n_heads, HEAD_W)
        rows = [_decode_attn(q.reshape(per_head), k.reshape(per_head), v.reshape(per_head),
                             slabs[0][grp], slabs[1][grp], grp * group, lam, g_sub, j, lam_init)
                for grp in range(n_groups)]
        return jnp.concatenate(rows, axis=0).reshape(n_dec, d)

    y_s, conv_s, k_s, v_s = run(x_sample.reshape(n_dec, d), cos_s, sin_s, 1,
                                ffn_sample, conv_sample, attn_sample, False)
    conv_s = jnp.stack(conv_s, axis=0).transpose(0, 2, 1, 3)

    return (y_p.reshape(bsz, s, d), y_s.reshape(n_dec, 1, d), conv_p,
            k_p.reshape(bsz, s, n_heads, HEAD_W), v_p.reshape(bsz, s, n_heads, HEAD_W),
            conv_s, k_s.reshape(n_dec, 1, n_heads, HEAD_W), v_s.reshape(n_dec, 1, n_heads, HEAD_W))
```

```python
import functools
import math

import jax
import jax.numpy as jnp
from jax import lax
from jax.experimental import pallas as pl
from jax.experimental.pallas import tpu as pltpu

HEAD_DIM = 64
HEAD_W = 2 * HEAD_DIM
CONV_WIDTH = 31
CONV_HALO = 32
ROPE_THETA = 10000.0
EPS = 1e-6
NEG_BIG = -1e30

TOKEN_TILE = 512
PROJ_TILE = 1024
FFN_CHUNK = 256
DECODE_SEQS = 2
CONV_ROWS = 8
CONV_TAIL = CONV_ROWS
VMEM_LIMIT = 60 * 1024 * 1024

F32 = jnp.float32
BF16 = jnp.bfloat16


def _params(n_axes, vmem=VMEM_LIMIT):
    return pltpu.CompilerParams(dimension_semantics=("arbitrary",) * n_axes,
                                vmem_limit_bytes=vmem)


def _rms(x, g):
    return x * lax.rsqrt(jnp.mean(x * x, axis=-1, keepdims=True) + EPS) * g


def _resident(shape, index_map):
    return pl.BlockSpec(shape, index_map, pipeline_mode=pl.Buffered(1))


def _compact_pages(page_refs, out_ref, n_pages):
    for idx, ref in enumerate(page_refs):
        sidx, p = divmod(idx, n_pages)
        rows = ref.shape[1]
        out_ref[sidx, p * rows:(p + 1) * rows, :] = ref[0].astype(BF16)


def _ffn_kernel(*refs, final_norm, attn_in, q_scale, n_page_refs, n_pages):
    refs = list(refs)
    if n_page_refs:
        refs.pop(0)
    x_ref, g_ref, wg_ref, wu_ref, wo_ref = refs[:5]
    del refs[:5]
    if final_norm:
        gf_ref = refs.pop(0)
    if attn_in:
        a_ref, wa_ref = refs[:2]
        del refs[:2]
    if q_scale is not None:
        gq_ref, wq_ref, cos_ref, sin_ref = refs[:4]
        del refs[:4]
    page_refs = refs[:n_page_refs]
    del refs[:n_page_refs]
    o_ref = refs.pop(0)
    if q_scale is not None:
        q_ref = refs.pop(0)
    if n_page_refs:
        _compact_pages(page_refs, refs.pop(0), n_pages)
    xn_ref, h_ref = refs
    x = x_ref[...]
    if attn_in:
        x = x + jnp.dot(a_ref[...], wa_ref[...], preferred_element_type=F32)
    xn_ref[...] = _rms(x, g_ref[...]).astype(BF16)
    d_ff = h_ref.shape[1]
    for c in range(0, d_ff, FFN_CHUNK):
        xn = xn_ref[...]
        gate = jnp.dot(xn, wg_ref[:, c:c + FFN_CHUNK], preferred_element_type=F32)
        up = jnp.dot(xn, wu_ref[:, c:c + FFN_CHUNK], preferred_element_type=F32)
        h_ref[:, c:c + FFN_CHUNK] = (gate * jax.nn.sigmoid(gate) * up).astype(BF16)
    y = x + 0.5 * jnp.dot(h_ref[...], wo_ref[...], preferred_element_type=F32)
    if final_norm:
        y = _rms(y, gf_ref[...])
    o_ref[...] = y
    if q_scale is not None:
        yq = jnp.dot(_rms(y, gq_ref[...]).astype(BF16), wq_ref[...], preferred_element_type=F32)
        cos = cos_ref[...]
        sin = sin_ref[...]
        for h in range(yq.shape[1] // HEAD_W):
            sl = slice(h * HEAD_W, (h + 1) * HEAD_W)
            q_ref[:, sl] = (_rope_head(yq[:, sl], cos, sin) * q_scale).astype(q_ref.dtype)


def _ffn(x, gains, w_in, w_out, l, i, final_gain=None, pages=None, attn=None, query=None):
    t, d = x.shape
    f = w_out.shape[2]
    assert f % FFN_CHUNK == 0
    tm = min(TOKEN_TILE, t)
    assert t % tm == 0
    steps = t // tm
    tok = pl.BlockSpec((tm, d), lambda r, *_: (r, 0))
    in_specs = [
        tok,
        _resident((None, None, 1, d), lambda r, *_: (l, i, 0, 0)),
        _resident((None, None, d, f), lambda r, *_: (l, i, 0, 0)),
        _resident((None, None, d, f), lambda r, *_: (l, i, 0, 1)),
        _resident((None, None, f, d), lambda r, *_: (l, i, 0, 0)),
    ]
    args = [x, gains, w_in, w_in, w_out]
    out_specs = [tok]
    out_shape = [jax.ShapeDtypeStruct((t, d), F32)]
    if final_gain is not None:
        in_specs.append(_resident((1, d), lambda r, *_: (0, 0)))
        args.append(final_gain)
    if attn is not None:
        a, wa, ja = attn
        in_specs += [tok, _resident((None, d, d), lambda r, *_: (ja, 0, 0))]
        args += [a, wa]
    q_scale = None
    if query is not None:
        gq, wq, jq, cos, sin, q_scale = query
        pos_tiles = cos.shape[0] // tm
        pos = pl.BlockSpec((tm, HEAD_W), lambda r, *_: (r % pos_tiles, 0))
        in_specs += [_resident((None, 1, d), lambda r, *_: (l, 0, 0)),
                     _resident((None, d, d), lambda r, *_: (jq, 0, 0)), pos, pos]
        args += [gq, wq, cos, sin]
        out_specs.append(tok)
        out_shape.append(jax.ShapeDtypeStruct((t, d), BF16))
    n_page_refs = n_pages = 0
    if pages is not None:
        page_table, cache, first_seq, n_seq = pages
        n_pages = page_table.shape[1]
        rows = cache.shape[1]
        assert n_seq % steps == 0
        per_step = n_seq // steps
        for sq in range(per_step):
            for p in range(n_pages):
                in_specs.append(pl.BlockSpec(
                    (1, rows, HEAD_W),
                    lambda r, pt, sq=sq, p=p: (pt[first_seq + r * per_step + sq, p], 0, 0)))
                args.append(cache)
        n_page_refs = per_step * n_pages
        out_specs.append(pl.BlockSpec((per_step, n_pages * rows, HEAD_W),
                                      lambda r, pt: (r, 0, 0)))
        out_shape.append(jax.ShapeDtypeStruct((n_seq, n_pages * rows, HEAD_W), BF16))
        args = [page_table] + args
    grid_spec = pltpu.PrefetchScalarGridSpec(
        num_scalar_prefetch=0 if pages is None else 1,
        grid=(steps,),
        in_specs=in_specs,
        out_specs=out_specs,
        scratch_shapes=[pltpu.VMEM((tm, d), BF16), pltpu.VMEM((tm, f), BF16)],
    )
    return pl.pallas_call(
        functools.partial(_ffn_kernel, final_norm=final_gain is not None,
                          attn_in=attn is not None, q_scale=q_scale,
                          n_page_refs=n_page_refs, n_pages=n_pages),
        grid_spec=grid_spec,
        out_shape=out_shape,
        compiler_params=_params(1),
        name="ffn",
    )(*args)


def _glu_kernel(x_ref, g_ref, w_ref, b_ref, u_ref):
    d = u_ref.shape[1]
    xn = _rms(x_ref[...], g_ref[...]).astype(BF16)
    y = jnp.dot(xn, w_ref[...], preferred_element_type=F32) + b_ref[...]
    u_ref[...] = y[:, :d] * jax.nn.sigmoid(y[:, d:])


def _glu_proj(x, gains, w, b, l):
    t, d = x.shape
    tm = min(PROJ_TILE, t)
    return pl.pallas_call(
        _glu_kernel,
        grid=(t // tm,),
        in_specs=[
            pl.BlockSpec((tm, d), lambda r: (r, 0)),
            _resident((None, 1, d), lambda r: (l, 0, 0)),
            _resident((None, d, 2 * d), lambda r: (l, 0, 0)),
            _resident((None, 1, 2 * d), lambda r: (l, 0, 0)),
        ],
        out_specs=pl.BlockSpec((tm, d), lambda r: (r, 0)),
        out_shape=jax.ShapeDtypeStruct((t, d), F32),
        compiler_params=_params(1),
        name="conv_glu",
    )(x, gains, w, b)


def _rope_head(xh, cos, sin_signed):
    half = HEAD_DIM // 2
    lane = lax.broadcasted_iota(jnp.int32, xh.shape, 1)
    partner = jnp.where(lane % HEAD_DIM < half,
                        pltpu.roll(xh, HEAD_W - half, 1),
                        pltpu.roll(xh, half, 1))
    return xh * cos + partner * sin_signed


def _kv_kernel(x_ref, g_ref, w_ref, cos_ref, sin_ref, k_ref, v_ref, *attn_refs):
    d = k_ref.shape[1]
    xn = _rms(x_ref[...], g_ref[...]).astype(BF16)
    y = jnp.dot(xn, w_ref[...], preferred_element_type=F32)
    v_ref[...] = y[:, d:]
    cos = cos_ref[...]
    sin = sin_ref[...]
    for h in range(d // HEAD_W):
        sl = slice(h * HEAD_W, (h + 1) * HEAD_W)
        kh = _rope_head(y[:, sl], cos, sin)
        k_ref[:, sl] = kh
        if attn_refs:
            kb_ref, vt_ref = attn_refs
            kb_ref[:, sl] = kh.astype(BF16)
            vt_ref[h] = y[:, d + h * HEAD_W:d + (h + 1) * HEAD_W].T.astype(BF16)


def _kv_proj(x, gain, w, cos, sin, pos_tiles, attn_layout):
    t, d = x.shape
    tm = min(TOKEN_TILE, t)
    tok = pl.BlockSpec((tm, d), lambda r: (r, 0))
    pos = pl.BlockSpec((tm, HEAD_W), lambda r: (r % pos_tiles, 0))
    out_specs = [tok, tok]
    out_shape = [jax.ShapeDtypeStruct((t, d), F32), jax.ShapeDtypeStruct((t, d), F32)]
    if attn_layout:
        n_heads = d // HEAD_W
        out_specs += [tok, pl.BlockSpec((None, n_heads, None, HEAD_W, tm),
                                        lambda r: (r // pos_tiles, 0, r % pos_tiles, 0, 0))]
        out_shape += [jax.ShapeDtypeStruct((t, d), BF16),
                      jax.ShapeDtypeStruct((t // (pos_tiles * tm), n_heads, pos_tiles, HEAD_W, tm),
                                           BF16)]
    return pl.pallas_call(
        _kv_kernel,
        grid=(t // tm,),
        in_specs=[tok, _resident((1, d), lambda r: (0, 0)),
                  _resident((d, 2 * d), lambda r: (0, 0)), pos, pos],
        out_specs=out_specs,
        out_shape=out_shape,
        compiler_params=_params(1),
        name="kv_proj",
    )(x, gain, w, cos, sin)


def _conv_tail(c, x, lng_ref, lnb_ref, w_ref, b_ref):
    mu = jnp.mean(c, axis=-1, keepdims=True)
    xc = c - mu
    y = xc * lax.rsqrt(jnp.mean(xc * xc, axis=-1, keepdims=True) + EPS)
    y = y * lng_ref[...] + lnb_ref[...]
    z = (y * jax.nn.sigmoid(y)).astype(BF16)
    return x + jnp.dot(z, w_ref[...], preferred_element_type=F32) + b_ref[...]


def _conv_seq_kernel(x_ref, g_ref, wi_ref, bi_ref, dw_ref, dwb_ref, lng_ref, lnb_ref, w_ref, b_ref,
                     o_ref, st_ref, full_ref, c_ref):
    ts, d = c_ref.shape
    first = pl.program_id(1) == 0

    @pl.when(first)
    def _():
        full_ref[0:CONV_HALO, :] = jnp.zeros((CONV_HALO, d), F32)

    @pl.when(jnp.logical_not(first))
    def _():
        full_ref[0:CONV_HALO, :] = full_ref[ts:ts + CONV_HALO, :]

    xn = _rms(x_ref[0], g_ref[...]).astype(BF16)
    y = jnp.dot(xn, wi_ref[...], preferred_element_type=F32) + bi_ref[...]
    full_ref[CONV_HALO:CONV_HALO + ts, :] = y[:, :d] * jax.nn.sigmoid(y[:, d:])
    full_ref[CONV_HALO + ts:, :] = jnp.zeros((CONV_TAIL, d), F32)

    @pl.when(pl.program_id(1) == pl.num_programs(1) - 1)
    def _():
        hist = st_ref.shape[1]
        st_ref[0] = full_ref[CONV_HALO + ts - hist:CONV_HALO + ts, :]
    lead = CONV_HALO - (CONV_WIDTH - 1)

    n_a = (lead + CONV_WIDTH - 1) // CONV_ROWS + 1
    sub = lax.broadcasted_iota(jnp.int32, (CONV_ROWS, HEAD_W), 0)

    for cb in range(d // HEAD_W):
        sl = slice(cb * HEAD_W, (cb + 1) * HEAD_W)
        taps = {}
        for a in range(n_a):
            for b in range(CONV_ROWS):
                j = CONV_ROWS * a + b - lead
                if 0 <= j < CONV_WIDTH:
                    taps[a, b] = jnp.broadcast_to(dw_ref[j:j + 1, sl], (CONV_ROWS, HEAD_W))
        bias = jnp.broadcast_to(dwb_ref[:, sl], (CONV_ROWS, HEAD_W))

        def partials(r):
            base = r * CONV_ROWS
            rows = [full_ref[pl.ds(base + CONV_ROWS * a, CONV_ROWS), sl] for a in range(n_a)]
            out = []
            for b in range(CONV_ROWS):
                z = None
                for a in range(n_a):
                    if (a, b) in taps:
                        term = rows[a] * taps[a, b]
                        z = term if z is None else z + term
                out.append(z)
            return tuple(out)

        def block(r, z_prev):
            z_next = partials(r)
            acc = bias + z_prev[0]
            for b in range(1, CONV_ROWS):
                mixed = jnp.where(sub >= b, z_prev[b], z_next[b])
                acc = acc + pltpu.roll(mixed, CONV_ROWS - b, 0)
            c_ref[pl.ds((r - 1) * CONV_ROWS, CONV_ROWS), sl] = acc
            return z_next

        z = partials(0)
        for r in range(1, ts // CONV_ROWS + 1):
            z = block(r, z)

    o_ref[0] = _conv_tail(c_ref[...], x_ref[0], lng_ref, lnb_ref, w_ref, b_ref)


def _conv_seq(x, gains, wi, bi, dw, dwb, lng, lnb, w, b, l):
    bsz, s, d = x.shape
    hist = CONV_WIDTH - 1
    ts = min(TOKEN_TILE, s)
    assert s % ts == 0 and ts >= CONV_HALO
    tile = pl.BlockSpec((1, ts, d), lambda bi, i: (bi, i, 0))
    vec = _resident((None, 1, d), lambda bi, i: (l, 0, 0))
    return pl.pallas_call(
        _conv_seq_kernel,
        grid=(bsz, s // ts),
        in_specs=[
            tile,
            vec,
            _resident((None, d, 2 * d), lambda bi, i: (l, 0, 0)),
            _resident((None, 1, 2 * d), lambda bi, i: (l, 0, 0)),
            _resident((None, CONV_WIDTH, d), lambda bi, i: (l, 0, 0)),
            vec, vec, vec,
            _resident((None, d, d), lambda bi, i: (l, 0, 0)),
            vec,
        ],
        out_specs=[tile, pl.BlockSpec((1, hist, d), lambda bi, i: (bi, 0, 0))],
        out_shape=[jax.ShapeDtypeStruct((bsz, s, d), F32),
                   jax.ShapeDtypeStruct((bsz, hist, d), F32)],
        scratch_shapes=[pltpu.VMEM((ts + CONV_HALO + CONV_TAIL, d), F32), pltpu.VMEM((ts, d), F32)],
        compiler_params=_params(2),
        name="conv_seq",
    )(x, gains, wi, bi, dw, dwb, lng, lnb, w, b)


def _conv_step_kernel(st_ref, u_ref, x_ref, dw_ref, dwb_ref, lng_ref, lnb_ref, w_ref, b_ref,
                      o_ref, ns_ref):
    hist = CONV_WIDTH - 1
    u = u_ref[...]
    c = u * dw_ref[hist:hist + 1, :] + dwb_ref[...]
    for j in range(hist):
        c = c + st_ref[j] * dw_ref[j:j + 1, :]
    ns_ref[0:hist - 1] = st_ref[1:hist]
    ns_ref[hist - 1] = u
    o_ref[...] = _conv_tail(c, x_ref[...], lng_ref, lnb_ref, w_ref, b_ref)


def _conv_step(state, u, x, dw, dwb, lng, lnb, w, b, l):
    n, d = u.shape
    nb = min(16, n)
    assert n % nb == 0
    hist = CONV_WIDTH - 1
    tok = pl.BlockSpec((nb, d), lambda r: (r, 0))
    vec = _resident((None, 1, d), lambda r: (l, 0, 0))
    return pl.pallas_call(
        _conv_step_kernel,
        grid=(n // nb,),
        in_specs=[
            pl.BlockSpec((None, hist, nb, d), lambda r: (l, 0, r, 0)),
            tok, tok,
            _resident((None, CONV_WIDTH, d), lambda r: (l, 0, 0)),
            vec, vec, vec,
            _resident((None, d, d), lambda r: (l, 0, 0)),
            vec,
        ],
        out_specs=[tok, pl.BlockSpec((hist, nb, d), lambda r: (0, r, 0))],
        out_shape=[jax.ShapeDtypeStruct((n, d), F32), jax.ShapeDtypeStruct((hist, n, d), F32)],
        compiler_params=_params(1),
        name="conv_step",
    )(state, u, x, dw, dwb, lng, lnb, w, b)


def _lambda(lam_ref, lam_init):
    lv = lam_ref[...]
    a = jnp.sum(lv[0:1] * lv[1:2], axis=-1, keepdims=True)
    b = jnp.sum(lv[2:3] * lv[3:4], axis=-1, keepdims=True)
    return jnp.exp(a) - jnp.exp(b) + lam_init


def _sub_norm(o, g, lam_init):
    return o * lax.rsqrt(jnp.mean(o * o, axis=-1, keepdims=True) + EPS) * g * (1.0 - lam_init)


def _flash_kernel(lam_ref, g_ref, q_ref, k_ref, vt_ref, o_ref, qq_ref, acc_ref, *, lam_init):
    tq = tk = vt_ref.shape[2]
    lam = _lambda(lam_ref, lam_init)

    def scores(js, mask_last, c):
        half = tq // 2
        pieces = []
        for n, j in enumerate(js):
            if mask_last and n == len(js) - 1:
                pieces += [(j, slice(0, half), slice(0, half), True),
                           (j, slice(half, tq), slice(0, tk), True)]
            else:
                pieces.append((j, slice(0, tq), slice(0, tk), False))
        ss = []
        for j, cols, keys, _ in pieces:
            kc = k_ref[0, pl.ds(j * tk + keys.start, keys.stop - keys.start), :]
            ss.append(lax.dot_general(kc, qq_ref[c, cols, :], (((1,), (1,)), ((), ())),
                                      preferred_element_type=F32))
        return pieces, ss, c

    def update(pieces, ss, c, carry):
        carry = [list(st) for st in carry]
        work = []
        for (j, cols, keys, masked), s in zip(pieces, ss):
            m_old, l_old = carry[c][0][:, cols], carry[c][1][:, cols]
            if masked:
                kpos = keys.start + lax.broadcasted_iota(jnp.int32, s.shape, 0)
                qpos = cols.start + lax.broadcasted_iota(jnp.int32, s.shape, 1)
                s = jnp.where(kpos <= qpos, s, NEG_BIG)
            m_new = jnp.maximum(m_old, jnp.max(s, axis=0, keepdims=True))
            alpha = jnp.exp2(m_old - m_new)
            p = jnp.exp2(s - m_new)
            l_new = alpha * l_old + jnp.sum(p, axis=0, keepdims=True)
            for idx, new in ((0, m_new), (1, l_new)):
                old = carry[c][idx]
                parts = ([old[:, :cols.start]] if cols.start else []) + [new]
                parts += [old[:, cols.stop:]] if cols.stop < tq else []
                carry[c][idx] = jnp.concatenate(parts, axis=1)
            work.append((j, cols, keys, alpha, p.astype(BF16)))
        for j, cols, keys, alpha, p in work:
            acc_ref[c, :, cols] = alpha * acc_ref[c, :, cols] + jnp.dot(
                vt_ref[j, :, keys], p, preferred_element_type=F32)
        return tuple(tuple(st) for st in carry)

    def query_tile(i):
        rows = pl.ds(i * tq, tq)
        q = q_ref[0, rows, :]
        lane = lax.broadcasted_iota(jnp.int32, q.shape, 1)
        zero = jnp.zeros_like(q)
        qq_ref[0] = jnp.where(lane < HEAD_DIM, q, zero)
        qq_ref[1] = jnp.where(lane >= HEAD_DIM, q, zero)
        acc_ref[...] = jnp.zeros(acc_ref.shape, F32)
        stat = (jnp.full((1, tq), NEG_BIG, F32), jnp.zeros((1, tq), F32))
        groups = [[2 * jj, 2 * jj + 1] for jj in range(i // 2)] + [[i - 1, i] if i % 2 else [i]]
        units = [(g, n == len(groups) - 1, c) for n, g in enumerate(groups) for c in range(2)]
        carry = (stat, stat)
        ahead = scores(*units[0])
        for n in range(len(units)):
            cur = ahead
            if n + 1 < len(units):
                ahead = scores(*units[n + 1])
            carry = update(*cur, carry)
        (_, l1), (_, l2) = carry
        ot = acc_ref[0] * (1.0 / l1) - lam * (acc_ref[1] * (1.0 / l2))
        ot = ot * lax.rsqrt(jnp.mean(ot * ot, axis=0, keepdims=True) + EPS)
        ot = ot * (g_ref[...] * (1.0 - lam_init))
        o_ref[0, rows, :] = ot.T.astype(o_ref.dtype)

    for i in range(q_ref.shape[1] // tq):
        query_tile(i)


def _flash_attn(q, k, vt, lam, g_col, j, lam_init):
    bsz, s, d = q.shape
    tile = vt.shape[4]
    assert s % tile == 0 and vt.shape[2] == s // tile
    head = pl.BlockSpec((1, s, HEAD_W), lambda bi, h: (bi, 0, h))
    return pl.pallas_call(
        functools.partial(_flash_kernel, lam_init=lam_init),
        grid=(bsz, d // HEAD_W),
        in_specs=[
            pl.BlockSpec((None, 4, HEAD_DIM), lambda bi, h: (j, 0, 0)),
            pl.BlockSpec((None, HEAD_W, 1), lambda bi, h: (j, 0, 0)),
            head, head,
            pl.BlockSpec((None, None, s // tile, HEAD_W, tile), lambda bi, h: (bi, h, 0, 0, 0)),
        ],
        out_specs=head,
        out_shape=jax.ShapeDtypeStruct((bsz, s, d), BF16),
        scratch_shapes=[pltpu.VMEM((2, tile, HEAD_W), BF16), pltpu.VMEM((2, HEAD_W, tile), F32)],
        compiler_params=_params(2),
        name="flash_diff_attn",
    )(lam, g_col, q, k, vt)


def _decode_kernel(lam_ref, g_ref, q_ref, kn_ref, vn_ref, k_ref, v_ref, o_ref, *, lam_init):
    n_heads = q_ref.shape[1]
    shape = (2 * n_heads, HEAD_W)
    own_half = (lax.broadcasted_iota(jnp.int32, shape, 1) // HEAD_DIM ==
                lax.broadcasted_iota(jnp.int32, shape, 0) // n_heads)
    lam = _lambda(lam_ref, lam_init)
    for sq in range(q_ref.shape[0]):
        q = q_ref[sq].astype(F32)
        qrows = jnp.where(own_half, jnp.concatenate([q, q], axis=0), 0.0)
        s = lax.dot_general(qrows.astype(BF16), k_ref[sq], (((1,), (1,)), ((), ())),
                            preferred_element_type=F32)
        own = (lax.broadcasted_iota(jnp.int32, s.shape, 1) % n_heads ==
               lax.broadcasted_iota(jnp.int32, s.shape, 0) % n_heads)
        s = jnp.where(own, s, NEG_BIG)
        kn = kn_ref[sq]
        vn = vn_ref[sq]
        s_new = jnp.sum(qrows * jnp.concatenate([kn, kn], axis=0), axis=1, keepdims=True)
        m = jnp.maximum(jnp.max(s, axis=1, keepdims=True), s_new)
        p = jnp.exp2(s - m)
        p_new = jnp.exp2(s_new - m)
        inv_l = 1.0 / (jnp.sum(p, axis=1, keepdims=True) + p_new)
        o = jnp.dot(p.astype(BF16), v_ref[sq], preferred_element_type=F32)
        o = (o + p_new * jnp.concatenate([vn, vn], axis=0)) * inv_l
        res = o[:n_heads] - lam * o[n_heads:]
        o_ref[sq] = _sub_norm(res, g_ref[...], lam_init).astype(o_ref.dtype)


def _decode_attn(q, k_new, v_new, k_slab, v_slab, first_seq, lam, g, j, lam_init):
    n, rows, _ = k_slab.shape
    n_heads = q.shape[1]
    per = DECODE_SEQS if n % DECODE_SEQS == 0 else 1
    assert first_seq % per == 0
    row = pl.BlockSpec((per, n_heads, HEAD_W), lambda b: (first_seq // per + b, 0, 0))
    slab = pl.BlockSpec((per, rows, HEAD_W), lambda b: (b, 0, 0))
    return pl.pallas_call(
        functools.partial(_decode_kernel, lam_init=lam_init),
        grid=(n // per,),
        in_specs=[
            pl.BlockSpec((None, 4, HEAD_DIM), lambda b: (j, 0, 0)),
            pl.BlockSpec((None, 1, HEAD_W), lambda b: (j, 0, 0)),
            row, row, row, slab, slab,
        ],
        out_specs=pl.BlockSpec((per, n_heads, HEAD_W), lambda b: (b, 0, 0)),
        out_shape=jax.ShapeDtypeStruct((n, n_heads, HEAD_W), BF16),
        compiler_params=_params(1),
        name="decode_attn",
    )(lam, g, q, k_new, v_new, k_slab, v_slab)


def _rope_tables(pos):
    half = HEAD_DIM // 2
    freqs = ROPE_THETA ** (-jnp.arange(half, dtype=F32) * 2.0 / HEAD_DIM)
    ang = pos.astype(F32)[:, None] * freqs[None, :]
    cos, sin = jnp.cos(ang), jnp.sin(ang)
    return jnp.tile(cos, (1, 4)), jnp.tile(jnp.concatenate([-sin, sin], axis=1), (1, 2))


def kernel(x_prompt, x_sample, state_conv, cache_k, cache_v, page_table, norm_ffn, ffn_w_in, ffn_w_out, norm_mix, conv_w_in, conv_b_in, conv_dw, conv_dw_b, conv_ln_g, conv_ln_b, conv_w_out, conv_b_out, kv_norm, w_kv, w_q, lambda_q1, lambda_k1, lambda_q2, lambda_k2, subln_g, w_o, norm_final):
    bsz, s, d = x_prompt.shape
    n_dec, dec_seq, _ = x_sample.shape
    assert dec_seq == 1
    depth = norm_mix.shape[0]
    n_a = conv_w_in.shape[0]
    n_heads = d // HEAD_W
    past_len = page_table.shape[1] * cache_k.shape[1]

    w_in = ffn_w_in.astype(BF16)
    w_out = ffn_w_out.astype(BF16)
    cw_in = conv_w_in.astype(BF16)
    cw_out = conv_w_out.astype(BF16)
    wkv = w_kv.astype(BF16)
    wq = w_q.astype(BF16)
    wo = w_o.astype(BF16)
    g_ffn = norm_ffn[:, :, None, :]
    g_mix = norm_mix[:, None, :]
    g_kv = kv_norm[None, :]
    g_fin = norm_final[None, :]
    cb_in = conv_b_in[:, None, :]
    cdw_b = conv_dw_b[:, None, :]
    cln_g = conv_ln_g[:, None, :]
    cln_b = conv_ln_b[:, None, :]
    cb_out = conv_b_out[:, None, :]
    lam = jnp.stack([lambda_q1, lambda_k1, lambda_q2, lambda_k2], axis=1)
    g_sub = subln_g[:, None, :]
    g_sub_col = subln_g[:, :, None]

    q_scale = HEAD_DIM ** -0.5 * math.log2(math.e)

    def run(x, cos, sin, pos_tiles, ffn_fn, conv_fn, attn_fn, attn_layout):
        conv_rows = []
        k = v = kv_ctx = None
        for l in range(depth):
            if l == n_a:
                kv_ctx = _kv_proj(x, g_kv, wkv, cos, sin, pos_tiles, attn_layout)
                k, v = kv_ctx[:2]
            if l < n_a:
                x = ffn_fn(x, l, 0)[0]
                x, state = conv_fn(x, l)
                conv_rows.append(state)
                x = ffn_fn(x, l, 1)[0]
            else:
                j = l - n_a
                lam_init = 0.8 - 0.6 * math.exp(-0.3 * l)
                x, q = ffn_fn(x, l, 0, query=(g_mix, wq, j, cos, sin, q_scale))[:2]
                o = attn_fn(q, kv_ctx, j, lam_init)
                x = ffn_fn(x, l, 1, attn=(o, wo, j))[0]
        return x, conv_rows, k, v

    def final_gain(l, i):
        return g_fin if (l == depth - 1 and i == 1) else None

    cos_p, sin_p = _rope_tables(jnp.arange(s, dtype=jnp.int32))
    tiles_p = s // min(TOKEN_TILE, s)

    n_groups = depth
    group = n_dec // n_groups
    assert n_dec % n_groups == 0
    pool, page = cache_k.shape[:2]
    caches = (cache_k.reshape(pool, page * n_heads, HEAD_W),
              cache_v.reshape(pool, page * n_heads, HEAD_W))
    slabs = ([], [])

    def ffn_prompt(x, l, i, **fused):
        call = 2 * l + i
        which, grp = divmod(call, n_groups)
        *outs, slab = _ffn(x, g_ffn, w_in, w_out, l, i, final_gain(l, i),
                           pages=(page_table, caches[which], grp * group, group), **fused)
        slabs[which].append(slab)
        return outs

    def conv_prompt(x, l):
        y, state = _conv_seq(x.reshape(bsz, s, d), g_mix, cw_in, cb_in, conv_dw, cdw_b, cln_g,
                             cln_b, cw_out, cb_out, l)
        return y.reshape(bsz * s, d), state

    def attn_prompt(q, kv_ctx, j, lam_init):
        _, _, kb, vt = kv_ctx
        o = _flash_attn(q.reshape(bsz, s, d), kb.reshape(bsz, s, d), vt, lam, g_sub_col, j,
                        lam_init)
        return o.reshape(bsz * s, d)

    y_p, conv_p, k_p, v_p = run(x_prompt.reshape(bsz * s, d), cos_p, sin_p, tiles_p,
                                ffn_prompt, conv_prompt, attn_prompt, True)
    conv_p = jnp.stack(conv_p, axis=0)

    cos_s, sin_s = _rope_tables(jnp.full((n_dec,), past_len, dtype=jnp.int32))

    state_hist = state_conv.transpose(0, 2, 1, 3)

    def conv_sample(x, l):
        u = _glu_proj(x, g_mix, cw_in, cb_in, l)
        return _conv_step(state_hist, u, x, conv_dw, cdw_b, cln_g, cln_b, cw_out, cb_out, l)

    def ffn_sample(x, l, i, **fused):
        return _ffn(x, g_ffn, w_in, w_out, l, i, final_gain(l, i), **fused)

    def attn_sample(q, kv_ctx, j, lam_init):
        k, v = kv_ctx
        per_head = (n_dec, n_heads, HEAD_W)
        rows = [_decode_attn(q.reshape(per_head), k.reshape(per_head), v.reshape(per_head),
                             slabs[0][grp], slabs[1][grp], grp * group, lam, g_sub, j, lam_init)
                for grp in range(n_groups)]
        return jnp.concatenate(rows, axis=0).reshape(n_dec, d)

    y_s, conv_s, k_s, v_s = run(x_sample.reshape(n_dec, d), cos_s, sin_s, 1,
                                ffn_sample, conv_sample, attn_sample, False)
    conv_s = jnp.stack(conv_s, axis=0).transpose(0, 2, 1, 3)

    return (y_p.reshape(bsz, s, d), y_s.reshape(n_dec, 1, d), conv_p,
            k_p.reshape(bsz, s, n_heads, HEAD_W), v_p.reshape(bsz, s, n_heads, HEAD_W),
            conv_s, k_s.reshape(n_dec, 1, n_heads, HEAD_W), v_s.reshape(n_dec, 1, n_heads, HEAD_W))
```
